```python
import math
import jax
import jax.numpy as jnp
from jax import lax
import numpy as np

D_MODEL = 1024
BATCH = 16
SEQ = 2048
DEPTH = 2
DEC_BATCH = 32
DEC_SEQ = 8
PAST_LEN = 16384
PAGE_SIZE = 128

N_MIXERS = 2
N_CONV_LAYERS = (DEPTH + 1) // 2
N_ATTN_LAYERS = DEPTH // 2
CONV_WIDTH = 31
N_HEADS = 16
HEAD_DIM = D_MODEL // N_HEADS
Q_BLOCK = 128
LOGIT_OFFSET = 6.0
QK_BIAS_NORM = math.sqrt(LOGIT_OFFSET * math.sqrt(HEAD_DIM))
N_EXPERTS = 32
TOP_K = 4
D_EXPERT = D_MODEL
SWIGLU_LIMIT = 7.0
SWIGLU_ALPHA = 1.702
MOE_BLOCK = 128
DEEPNORM_ALPHA = (2 * DEPTH) ** 0.25
DEEPNORM_BETA = (8 * DEPTH) ** -0.25
LN_EPS = 1e-5

kernel_name = 'conformer_conv_stickbreak_moe_deepnorm_step'


def layer_norm(x, g, b):
    xf = x.astype(jnp.float32)
    mu = jnp.mean(xf, axis=-1, keepdims=True)
    var = jnp.mean(jnp.square(xf - mu), axis=-1, keepdims=True)
    return ((xf - mu) * lax.rsqrt(var + LN_EPS) * g.astype(jnp.float32) + b.astype(jnp.float32)).astype(x.dtype)


def conv_glu(x, w_pw1, b_pw1):
    h = x @ w_pw1 + b_pw1
    a, g = jnp.split(h, 2, axis=-1)
    return a * jax.nn.sigmoid(g)


def conv_tail(u_hist, dw_w, dw_b, cn_g, cn_b, w_pw2, b_pw2):
    y = lax.conv_general_dilated(u_hist, dw_w[:, None, :], window_strides=(1,), padding='VALID',
                                 dimension_numbers=('NWC', 'WIO', 'NWC'),
                                 feature_group_count=D_MODEL) + dw_b
    y = jax.nn.silu(layer_norm(y, cn_g, cn_b))
    return y @ w_pw2 + b_pw2


def stick_breaking_weights(z, key_pos, query_pos):
    mask = key_pos[None, :] < query_pos[:, None]
    log_beta = jax.nn.log_sigmoid(z)
    log_keep = jnp.where(mask, jax.nn.log_sigmoid(-z), 0.0)
    ax = z.ndim - 1
    later = lax.cumsum(log_keep, axis=ax, reverse=True) - log_keep
    return jnp.where(mask, jnp.exp(log_beta + later), 0.0)


def split_qkv(x, w_qkv, b_qkv):
    b, t, _ = x.shape
    qkv = (x @ w_qkv + b_qkv).reshape(b, t, 3, N_HEADS, HEAD_DIM)
    return qkv[:, :, 0], qkv[:, :, 1], qkv[:, :, 2]


def sb_attention_prompt(x, w_qkv, b_qkv, w_o):
    b, s, _ = x.shape
    q, k, v = split_qkv(x, w_qkv, b_qkv)
    scale = HEAD_DIM ** -0.5
    blk = min(Q_BLOCK, s)
    n_blk = s // blk
    key_pos = jnp.arange(s)

    def one_block(i):
        q_b = lax.dynamic_slice_in_dim(q, i * blk, blk, axis=1)
        z = jnp.einsum('bqhd,bkhd->bhqk', q_b, k, preferred_element_type=jnp.float32) * scale
        a = stick_breaking_weights(z, key_pos, i * blk + jnp.arange(blk))
        return jnp.einsum('bhqk,bkhd->bqhd', a.astype(v.dtype), v)

    o = lax.map(one_block, jnp.arange(n_blk))
    o = jnp.moveaxis(o, 0, 1).reshape(b, s, D_MODEL)
    return o @ w_o, k, v


def sb_attention_sample(x, k_past, v_past, w_qkv, b_qkv, w_o):
    bd, t, _ = x.shape
    past = k_past.shape[1]
    q, k, v = split_qkv(x, w_qkv, b_qkv)
    scale = HEAD_DIM ** -0.5
    z = jnp.concatenate([
        jnp.einsum('bqhd,bkhd->bhqk', q, k_past, preferred_element_type=jnp.float32),
        jnp.einsum('bqhd,bkhd->bhqk', q, k, preferred_element_type=jnp.float32)], axis=-1) * scale
    a = stick_breaking_weights(z, jnp.arange(past + t), past + jnp.arange(t)).astype(v.dtype)
    o = (jnp.einsum('bhqk,bkhd->bqhd', a[..., :past], v_past)
         + jnp.einsum('bhqk,bkhd->bqhd', a[..., past:], v))
    return o.reshape(bd, t, D_MODEL) @ w_o, k, v


def moe(x, w_router, b_router, w1, b1, w2, b2):
    n, d = x.shape
    logits = (x @ w_router + b_router).astype(jnp.float32)
    top_val, top_idx = lax.top_k(logits, TOP_K)
    gates = jax.nn.softmax(top_val, axis=-1).astype(x.dtype)
    flat_e = top_idx.reshape(-1)
    flat_tok = jnp.arange(n * TOP_K, dtype=jnp.int32) // TOP_K
    flat_gate = gates.reshape(-1)
    n_assign = n * TOP_K
    order = jnp.argsort(flat_e)
    sorted_e = flat_e[order]
    counts = jnp.bincount(flat_e, length=N_EXPERTS)
    padded = (counts + MOE_BLOCK - 1) // MOE_BLOCK * MOE_BLOCK
    start = jnp.cumsum(counts) - counts
    pstart = jnp.cumsum(padded) - padded
    dest = pstart[sorted_e] + jnp.arange(n_assign) - start[sorted_e]
    n_blocks = -(-(n_assign + N_EXPERTS * (MOE_BLOCK - 1)) // MOE_BLOCK)
    cap = n_blocks * MOE_BLOCK
    tok_slot = jnp.full((cap,), n, jnp.int32).at[dest].set(flat_tok[order])
    gate_slot = jnp.zeros((cap,), x.dtype).at[dest].set(flat_gate[order])
    block_e = jnp.minimum(jnp.searchsorted(jnp.cumsum(padded), jnp.arange(n_blocks) * MOE_BLOCK, side='right'),
                          N_EXPERTS - 1)
    x_pad = jnp.concatenate([x, jnp.zeros((1, d), x.dtype)], axis=0)

    def expert_block(args):
        e, toks, g = args
        h = x_pad[toks] @ w1[e] + b1[e]
        gate, up = jnp.split(h, 2, axis=-1)
        gate = jnp.minimum(gate, SWIGLU_LIMIT)
        up = jnp.clip(up, -SWIGLU_LIMIT, SWIGLU_LIMIT)
        act = (up + 1) * gate * jax.nn.sigmoid(SWIGLU_ALPHA * gate)
        return (act @ w2[e] + b2[e]) * g[:, None]

    out = lax.map(expert_block, (block_e, tok_slot.reshape(n_blocks, MOE_BLOCK),
                                 gate_slot.reshape(n_blocks, MOE_BLOCK)))
    y = jnp.zeros((n + 1, d), x.dtype).at[tok_slot].add(out.reshape(cap, d))
    return y[:n]


def setup_inputs(seed: int = 0) -> dict:
    key = jax.random.key(seed)
    ks = jax.random.split(key, 29)
    n_pages = PAST_LEN // PAGE_SIZE
    n_used = DEC_BATCH * n_pages
    n_pool = n_used + max(1, n_used // 4)
    d, f, e = D_MODEL, D_EXPERT, N_EXPERTS

    def nrm(k, shape, scale=1.0):
        return jax.random.normal(k, shape, jnp.float32) * scale

    u = nrm(ks[26], (N_ATTN_LAYERS, N_HEADS, HEAD_DIM))
    u = u / jnp.linalg.norm(u, axis=-1, keepdims=True) * QK_BIAS_NORM
    b_q = u.reshape(N_ATTN_LAYERS, d) + nrm(ks[27], (N_ATTN_LAYERS, d), 0.02)
    b_k = -u.reshape(N_ATTN_LAYERS, d) + nrm(ks[28], (N_ATTN_LAYERS, d), 0.02)
    b_v = nrm(ks[25], (N_ATTN_LAYERS, d), 0.02)
    b_qkv = jnp.concatenate([b_q, b_k, b_v], axis=-1)
    k_bias = b_k.reshape(N_ATTN_LAYERS, 1, 1, N_HEADS, HEAD_DIM)

    page_table = jax.random.permutation(ks[0], n_pool)[:n_used].reshape(DEC_BATCH, n_pages).astype(jnp.int32)
    return {
        'x_prompt': nrm(ks[1], (BATCH, SEQ, d)),
        'x_sample': nrm(ks[2], (DEC_BATCH, DEC_SEQ, d)),
        'state_conv': nrm(ks[3], (N_CONV_LAYERS, DEC_BATCH, CONV_WIDTH - 1, d), 0.5),
        'cache_k': nrm(ks[4], (N_ATTN_LAYERS, n_pool, PAGE_SIZE, N_HEADS, HEAD_DIM)) + k_bias,
        'cache_v': nrm(ks[5], (N_ATTN_LAYERS, n_pool, PAGE_SIZE, N_HEADS, HEAD_DIM)),
        'page_table': page_table,
        'w_pw1': nrm(ks[6], (N_CONV_LAYERS, d, 2 * d), d ** -0.5),
        'b_pw1': nrm(ks[7], (N_CONV_LAYERS, 2 * d), 0.02),
        'dw_w': nrm(ks[8], (N_CONV_LAYERS, CONV_WIDTH, d), CONV_WIDTH ** -0.5),
        'dw_b': nrm(ks[9], (N_CONV_LAYERS, d), 0.02),
        'cn_g': 1.0 + nrm(ks[10], (N_CONV_LAYERS, d), 0.02),
        'cn_b': nrm(ks[11], (N_CONV_LAYERS, d), 0.02),
        'w_pw2': nrm(ks[12], (N_CONV_LAYERS, d, d), d ** -0.5 * DEEPNORM_BETA),
        'b_pw2': nrm(ks[13], (N_CONV_LAYERS, d), 0.02),
        'w_qkv': nrm(ks[14], (N_ATTN_LAYERS, d, 3 * d), d ** -0.5),
        'b_qkv': b_qkv,
        'w_o': nrm(ks[15], (N_ATTN_LAYERS, d, d), d ** -0.5 * DEEPNORM_BETA),
        'ln_mix_g': 1.0 + nrm(ks[16], (DEPTH, d), 0.02),
        'ln_mix_b': nrm(ks[17], (DEPTH, d), 0.02),
        'ln_ffn_g': 1.0 + nrm(ks[18], (DEPTH, d), 0.02),
        'ln_ffn_b': nrm(ks[19], (DEPTH, d), 0.02),
        'w_router': nrm(ks[20], (DEPTH, d, e), d ** -0.5),
        'b_router': nrm(ks[21], (DEPTH, e), 0.01),
        'w_e1': nrm(ks[22], (DEPTH, e, d, 2 * f), d ** -0.5),
        'b_e1': nrm(ks[23], (DEPTH, e, 2 * f), 0.02),
        'w_e2': nrm(ks[24], (DEPTH, e, f, d), f ** -0.5 * DEEPNORM_BETA),
        'b_e2': nrm(jax.random.fold_in(ks[24], 1), (DEPTH, e, d), 0.02),
    }


def reference(x_prompt, x_sample, state_conv, cache_k, cache_v, page_table,
              w_pw1, b_pw1, dw_w, dw_b, cn_g, cn_b, w_pw2, b_pw2, w_qkv, b_qkv, w_o,
              ln_mix_g, ln_mix_b, ln_ffn_g, ln_ffn_b,
              w_router, b_router, w_e1, b_e1, w_e2, b_e2):
    b, s, d = x_prompt.shape
    bd, t, _ = x_sample.shape
    hp, hs = x_prompt, x_sample
    conv_p, conv_s, k_p, v_p, k_s, v_s = [], [], [], [], [], []
    for i in range(DEPTH):
        j = i // N_MIXERS
        if i % N_MIXERS == 0:
            u_p = jnp.pad(conv_glu(hp, w_pw1[j], b_pw1[j]), ((0, 0), (CONV_WIDTH - 1, 0), (0, 0)))
            u_s = jnp.concatenate([state_conv[j].astype(hs.dtype), conv_glu(hs, w_pw1[j], b_pw1[j])], axis=1)
            mp = conv_tail(u_p, dw_w[j], dw_b[j], cn_g[j], cn_b[j], w_pw2[j], b_pw2[j])
            ms = conv_tail(u_s, dw_w[j], dw_b[j], cn_g[j], cn_b[j], w_pw2[j], b_pw2[j])
            conv_p.append(u_p[:, -(CONV_WIDTH - 1):])
            conv_s.append(u_s[:, -(CONV_WIDTH - 1):])
        else:
            mp, kp_new, vp_new = sb_attention_prompt(hp, w_qkv[j], b_qkv[j], w_o[j])
            k_past = cache_k[j][page_table].reshape(bd, -1, N_HEADS, HEAD_DIM).astype(hs.dtype)
            v_past = cache_v[j][page_table].reshape(bd, -1, N_HEADS, HEAD_DIM).astype(hs.dtype)
            ms, ks_new, vs_new = sb_attention_sample(hs, k_past, v_past, w_qkv[j], b_qkv[j], w_o[j])
            k_p.append(kp_new)
            v_p.append(vp_new)
            k_s.append(ks_new)
            v_s.append(vs_new)
        hp = layer_norm(DEEPNORM_ALPHA * hp + mp, ln_mix_g[i], ln_mix_b[i])
        hs = layer_norm(DEEPNORM_ALPHA * hs + ms, ln_mix_g[i], ln_mix_b[i])
        flat = jnp.concatenate([hp.reshape(b * s, d), hs.reshape(bd * t, d)], axis=0)
        fo = moe(flat, w_router[i], b_router[i], w_e1[i], b_e1[i], w_e2[i], b_e2[i])
        hp = layer_norm(DEEPNORM_ALPHA * hp + fo[:b * s].reshape(b, s, d), ln_ffn_g[i], ln_ffn_b[i])
        hs = layer_norm(DEEPNORM_ALPHA * hs + fo[b * s:].reshape(bd, t, d), ln_ffn_g[i], ln_ffn_b[i])
    return (hp, hs, jnp.stack(conv_p), jnp.stack(conv_s), jnp.stack(k_p), jnp.stack(v_p), jnp.stack(k_s), jnp.stack(v_s))
```

```python
import functools
import math

import jax
import jax.numpy as jnp
from jax import lax
from jax.experimental import pallas as pl
from jax.experimental.pallas import tpu as pltpu

F32 = jnp.float32
BF16 = jnp.bfloat16
I32 = jnp.int32

TOP_K = 4
SWIGLU_LIMIT = 7.0
SWIGLU_ALPHA = 1.702
LN_EPS = 1e-5
LANES = 128
SUBLANES = 8
VMEM_LIMIT = 56 * 1024 * 1024
EXPERT_BLOCK = 256
NEG_BIG = -1e30


def _pick(n, cands):
    for c in cands:
        if n % c == 0:
            return c
    raise ValueError(f"no tile in {cands} divides {n}")


def _cparams(sem):
    return pltpu.CompilerParams(dimension_semantics=sem, vmem_limit_bytes=VMEM_LIMIT)


def _layer_norm(x, g, b):
    mu = jnp.mean(x, axis=-1, keepdims=True)
    xc = x - mu
    var = jnp.mean(xc * xc, axis=-1, keepdims=True)
    return xc * lax.rsqrt(var + LN_EPS) * g + b


def _mm_glu_kernel(x_ref, w_ref, b_ref, u_ref):
    d = u_ref.shape[-1]
    h = jnp.dot(x_ref[...].astype(BF16), w_ref[...], preferred_element_type=F32) + b_ref[...]
    u_ref[...] = h[:, :d] * jax.nn.sigmoid(h[:, d:])


def _mm_glu(x, w_bf, b):
    n, d = x.shape
    tm = _pick(n, (512, 256, 128, 64, 32, 16, 8))
    return pl.pallas_call(
        _mm_glu_kernel,
        out_shape=jax.ShapeDtypeStruct((n, d), F32),
        grid=(n // tm,),
        in_specs=[pl.BlockSpec((tm, d), lambda i: (i, 0)),
                  pl.BlockSpec((d, 2 * d), lambda i: (0, 0)),
                  pl.BlockSpec((1, 2 * d), lambda i: (0, 0))],
        out_specs=pl.BlockSpec((tm, d), lambda i: (i, 0)),
        compiler_params=_cparams(("arbitrary",)),
        name="mm_glu",
    )(x, w_bf, b.reshape(1, -1))


def _mm_res_ln_kernel(w_ref, b_ref, g_ref, beta_ref, *rest, alpha, bounds):
    o_ref = rest[-1]
    i = pl.program_id(0)
    for s, (lo, hi) in enumerate(bounds):
        a_ref, res_ref = rest[2 * s], rest[2 * s + 1]

        @pl.when(jnp.logical_and(i >= lo, i < hi))
        def _(a_ref=a_ref, res_ref=res_ref):
            m = jnp.dot(a_ref[...].astype(BF16), w_ref[...], preferred_element_type=F32) + b_ref[...]
            o_ref[...] = _layer_norm(alpha * res_ref[...] + m, g_ref[...], beta_ref[...])


def _mm_res_ln(segs, w_bf, b, g, beta, alpha):
    k, d = w_bf.shape
    tm = functools.reduce(math.gcd, [x for a, _, off in segs for x in (a.shape[0], off)])
    tm = _pick(tm, (512, 256, 128, 64, 32, 16, 8))
    vec = pl.BlockSpec((1, d), lambda i: (0, 0))
    in_specs = [pl.BlockSpec((k, d), lambda i: (0, 0)), vec, vec, vec]
    args = [w_bf, b.reshape(1, -1), g.reshape(1, -1), beta.reshape(1, -1)]
    bounds, lo = [], 0
    for a, res, off in segs:
        nt = a.shape[0] // tm
        in_specs.append(pl.BlockSpec((tm, k), lambda i, lo=lo, nt=nt: (jnp.clip(i - lo, 0, nt - 1), 0)))
        in_specs.append(pl.BlockSpec((tm, d), lambda i, lo=lo, nt=nt, so=off // tm:
                                     (jnp.clip(i - lo, 0, nt - 1) + so, 0)))
        args += [a, res]
        bounds.append((lo, lo + nt))
        lo += nt
    return pl.pallas_call(
        functools.partial(_mm_res_ln_kernel, alpha=alpha, bounds=tuple(bounds)),
        out_shape=jax.ShapeDtypeStruct((lo * tm, d), F32),
        grid=(lo,),
        in_specs=in_specs,
        out_specs=pl.BlockSpec((tm, d), lambda i: (i, 0)),
        compiler_params=_cparams(("arbitrary",)),
        name="mm_res_ln",
    )(*args)


def _mm_qkv_kernel(h_ref, w_ref, b_ref, q_ref, k_ref, v_ref):
    d = k_ref.shape[-1]
    r = jnp.dot(h_ref[...].astype(BF16), w_ref[...], preferred_element_type=F32) + b_ref[...]
    q_ref[...] = r[:, :d].astype(BF16)
    k_ref[...] = r[:, d:2 * d]
    v_ref[...] = r[:, 2 * d:]


def _mm_qkv(h, w_bf, b, n, row_off):
    d = h.shape[1]
    tm = _pick(math.gcd(n, row_off) if row_off else n, (512, 256, 128, 64, 32, 16, 8))
    ro = row_off // tm
    return pl.pallas_call(
        _mm_qkv_kernel,
        out_shape=(jax.ShapeDtypeStruct((n, d), BF16),
                   jax.ShapeDtypeStruct((n, d), F32),
                   jax.ShapeDtypeStruct((n, d), F32)),
        grid=(n // tm,),
        in_specs=[pl.BlockSpec((tm, d), lambda i: (i + ro, 0)),
                  pl.BlockSpec((d, 3 * d), lambda i: (0, 0)),
                  pl.BlockSpec((1, 3 * d), lambda i: (0, 0))],
        out_specs=(pl.BlockSpec((tm, d), lambda i: (i, 0)),) * 3,
        compiler_params=_cparams(("arbitrary",)),
        name="mm_qkv",
    )(h, w_bf, b.reshape(1, -1))


def _conv_kernel(*refs, width, hist, has_state, n_tiles):
    if has_state:
        u_ref, st_ref, w_ref, wb_ref, g_ref, b_ref, y_ref, cs_ref, buf = refs
    else:
        u_ref, w_ref, wb_ref, g_ref, b_ref, y_ref, cs_ref, buf = refs
    t = u_ref.shape[1]
    ti = pl.program_id(1)

    @pl.when(ti == 0)
    def _():
        if has_state:
            buf[0:hist, :] = st_ref[0]
        else:
            buf[0:hist, :] = jnp.zeros((hist, buf.shape[1]), F32)

    buf[hist:hist + t, :] = u_ref[0]
    lo = hist - (width - 1)
    acc = jnp.zeros((t, buf.shape[1]), F32) + wb_ref[...]
    for k in range(width):
        acc = acc + buf[lo + k:lo + k + t, :] * w_ref[k:k + 1, :]
    y = _layer_norm(acc, g_ref[...], b_ref[...])
    y_ref[0] = (y * jax.nn.sigmoid(y)).astype(BF16)
    cs_ref[0] = buf[t + lo:t + hist, :]
    if n_tiles > 1:
        buf[0:hist, :] = buf[t:t + hist, :]


def _conv(u, state, dw_w, dw_b, cn_g, cn_b):
    b, s, d = u.shape
    width = dw_w.shape[0]
    hist = -(-(width - 1) // SUBLANES) * SUBLANES
    t = _pick(s, (512, 256, 128, 64, 32, 16, 8))
    n_tiles = s // t
    assert n_tiles == 1 or t >= hist
    vec = pl.BlockSpec((1, d), lambda i, j: (0, 0))
    in_specs = [pl.BlockSpec((1, t, d), lambda i, j: (i, j, 0))]
    args = [u]
    if state is not None:
        in_specs.append(pl.BlockSpec((1, hist, d), lambda i, j: (i, 0, 0)))
        args.append(state)
    in_specs += [pl.BlockSpec((width, d), lambda i, j: (0, 0)), vec, vec, vec]
    args += [dw_w, dw_b.reshape(1, -1), cn_g.reshape(1, -1), cn_b.reshape(1, -1)]
    return pl.pallas_call(
        functools.partial(_conv_kernel, width=width, hist=hist, has_state=state is not None,
                          n_tiles=n_tiles),
        out_shape=(jax.ShapeDtypeStruct((b, s, d), BF16),
                   jax.ShapeDtypeStruct((b, width - 1, d), F32)),
        grid=(b, n_tiles),
        in_specs=in_specs,
        out_specs=(pl.BlockSpec((1, t, d), lambda i, j: (i, j, 0)),
                   pl.BlockSpec((1, width - 1, d), lambda i, j: (i, 0, 0))),
        scratch_shapes=[pltpu.VMEM((hist + t, d), F32)],
        compiler_params=_cparams(("arbitrary", "arbitrary")),
        name="conv_ln_swish",
    )(*args)


def _sb_block(z, mask, upper, run):
    soft = jnp.log(1.0 + jnp.exp(-jnp.abs(z)))
    log_keep = -(jnp.maximum(z, 0.0) + soft)
    log_beta = jnp.minimum(z, 0.0) - soft
    if mask is not None:
        log_keep = jnp.where(mask, log_keep, 0.0)
    later = jnp.dot(log_keep.astype(BF16), upper, preferred_element_type=F32)
    a = jnp.exp(log_beta + later + run)
    if mask is not None:
        a = jnp.where(mask, a, 0.0)
    return a.astype(BF16), run + jnp.sum(log_keep, axis=1, keepdims=True)


def _attn_prompt_kernel(q_ref, k_ref, v_ref, o_ref, kb, vb, *, scale, dh):
    qi = pl.program_id(2)
    blk, width = q_ref.shape
    heads = width // dh

    @pl.when(qi == 0)
    def _():
        kb[...] = k_ref[...].astype(BF16)
        vb[...] = v_ref[...].astype(BF16)

    q = q_ref[...]
    lane = lax.broadcasted_iota(I32, (blk, width), 1)
    row = lax.broadcasted_iota(I32, (blk, blk), 0)
    col = lax.broadcasted_iota(I32, (blk, blk), 1)
    upper = (row > col).astype(BF16)
    causal = col < row
    q_heads = [jnp.where((lane >= h * dh) & (lane < (h + 1) * dh), q, jnp.zeros_like(q))
               for h in range(heads)]

    def block(j, mask, carry):
        off = pl.multiple_of(j * blk, blk)
        ks = kb[pl.ds(off, blk), :]
        vs = vb[pl.ds(off, blk), :]
        out = []
        for h in range(heads):
            acc, run = carry[h]
            z = lax.dot_general(q_heads[h], ks, (((1,), (1,)), ((), ())),
                                preferred_element_type=F32) * scale
            a, run = _sb_block(z, mask, upper, run)
            out.append((acc + jnp.dot(a, vs, preferred_element_type=F32), run))
        return tuple(out)

    init = tuple((jnp.zeros((blk, width), F32), jnp.zeros((blk, 1), F32)) for _ in range(heads))
    carry = block(qi, causal, init)
    carry = lax.fori_loop(0, qi, lambda jj, c: block(qi - 1 - jj, None, c), carry)
    out = carry[0][0]
    for h in range(1, heads):
        out = jnp.where(lane >= h * dh, carry[h][0], out)
    o_ref[...] = out.astype(BF16)


def _attn_prompt(q, k, v, batch, seq, dh):
    n, d = q.shape
    width = LANES
    assert width % dh == 0 and d % width == 0
    blk = _pick(seq, (256, 128))
    nq = seq // blk
    return pl.pallas_call(
        functools.partial(_attn_prompt_kernel, scale=dh ** -0.5, dh=dh),
        out_shape=jax.ShapeDtypeStruct((n, d), BF16),
        grid=(batch, d // width, nq),
        in_specs=[pl.BlockSpec((blk, width), lambda b, h, i: (b * nq + i, h)),
                  pl.BlockSpec((seq, width), lambda b, h, i: (b, h)),
                  pl.BlockSpec((seq, width), lambda b, h, i: (b, h))],
        out_specs=pl.BlockSpec((blk, width), lambda b, h, i: (b * nq + i, h)),
        scratch_shapes=[pltpu.VMEM((seq, width), BF16), pltpu.VMEM((seq, width), BF16)],
        compiler_params=_cparams(("arbitrary", "arbitrary", "arbitrary")),
        name="attn_prompt",
    )(q, k, v)


def _attn_sample_kernel(pt_ref, q_ref, kn_ref, vn_ref, *rest, scale, dh, n_heads, ppb):
    del pt_ref
    k_refs, v_refs = rest[:ppb], rest[ppb:2 * ppb]
    o_ref, acc, run, qx = rest[2 * ppb:]
    p = pl.program_id(1)
    t, d = q_ref.shape
    page = k_refs[0].shape[1]
    m = n_heads * t
    rowh = lax.broadcasted_iota(I32, (m, d), 0) // t
    colh = lax.broadcasted_iota(I32, (m, d), 1) // dh
    own = rowh == colh

    def upper_of(n):
        return (lax.broadcasted_iota(I32, (n, n), 0) > lax.broadcasted_iota(I32, (n, n), 1)).astype(BF16)

    def step(kc, vc, mask):
        z = lax.dot_general(qx[...], kc, (((1,), (1,)), ((), ())), preferred_element_type=F32) * scale
        a, new_run = _sb_block(z, mask, upper_of(kc.shape[0]), run[...])
        acc[...] += jnp.dot(a, vc, preferred_element_type=F32)
        run[...] = new_run

    @pl.when(p == 0)
    def _():
        qt = jnp.concatenate([q_ref[...]] * n_heads, axis=0)
        qx[...] = jnp.where(own, qt, jnp.zeros_like(qt))
        acc[...] = jnp.zeros_like(acc)
        run[...] = jnp.zeros_like(run)
        pad = jnp.zeros((page - t, d), BF16)
        kc = jnp.concatenate([kn_ref[...].astype(BF16), pad], axis=0)
        vc = jnp.concatenate([vn_ref[...].astype(BF16), pad], axis=0)
        key = lax.broadcasted_iota(I32, (m, page), 1)
        qpos = lax.broadcasted_iota(I32, (m, page), 0) % t
        step(kc, vc, key < qpos)

    kc = jnp.concatenate([k_refs[r][0].astype(BF16) for r in reversed(range(ppb))], axis=0)
    vc = jnp.concatenate([v_refs[r][0].astype(BF16) for r in reversed(range(ppb))], axis=0)
    step(kc, vc, None)

    @pl.when(p == pl.num_programs(1) - 1)
    def _():
        sel = jnp.where(own, acc[...], 0.0).reshape(n_heads, t, d)
        o_ref[...] = jnp.sum(sel, axis=0).astype(BF16)


def _attn_sample(q, k_new, v_new, cache_k, cache_v, page_table, dh):
    n, d = q.shape
    db, n_pages = page_table.shape
    t = n // db
    page = cache_k.shape[1]
    n_heads = d // dh
    ppb = _pick(n_pages, (2, 1))
    steps = n_pages // ppb
    assert t <= page

    def page_spec(r):
        return pl.BlockSpec((1, page, d), lambda b, p, pt: (pt[b, n_pages - 1 - (p * ppb + r)], 0, 0))

    tok = pl.BlockSpec((t, d), lambda b, p, pt: (b, 0))
    grid_spec = pltpu.PrefetchScalarGridSpec(
        num_scalar_prefetch=1,
        grid=(db, steps),
        in_specs=[tok, tok, tok] + [page_spec(r) for r in range(ppb)] * 2,
        out_specs=tok,
        scratch_shapes=[pltpu.VMEM((n_heads * t, d), F32), pltpu.VMEM((n_heads * t, 1), F32),
                        pltpu.VMEM((n_heads * t, d), BF16)],
    )
    return pl.pallas_call(
        functools.partial(_attn_sample_kernel, scale=dh ** -0.5, dh=dh, n_heads=n_heads, ppb=ppb),
        out_shape=jax.ShapeDtypeStruct((n, d), BF16),
        grid_spec=grid_spec,
        compiler_params=_cparams(("arbitrary", "arbitrary")),
        name="attn_sample",
    )(page_table, q, k_new, v_new, *([cache_k] * ppb), *([cache_v] * ppb))


def _router_kernel(h_ref, w_ref, b_ref, idx_ref, gate_ref, rank_ref, cnt_ref, carry):
    i = pl.program_id(0)

    @pl.when(i == 0)
    def _():
        carry[...] = jnp.zeros_like(carry)

    logits = jnp.dot(h_ref[...], w_ref[...], preferred_element_type=F32,
                     precision=lax.Precision.HIGHEST) + b_ref[...]
    t, ep = logits.shape
    lane = lax.broadcasted_iota(I32, (t, ep), 1)
    left = logits
    chosen = jnp.zeros((t, ep), F32)
    vals, idxs = [], []
    for _ in range(TOP_K):
        top = jnp.max(left, axis=-1, keepdims=True)
        idx = jnp.min(jnp.where(left == top, lane, ep), axis=-1, keepdims=True)
        sel = lane == idx
        vals.append(top)
        idxs.append(idx)
        chosen = jnp.where(sel, 1.0, chosen)
        left = jnp.where(sel, -jnp.inf, left)
    exps = [jnp.exp(v - vals[0]) for v in vals]
    denom = exps[0]
    for e in exps[1:]:
        denom = denom + e
    lower = (lax.broadcasted_iota(I32, (t, t), 0) > lax.broadcasted_iota(I32, (t, t), 1)).astype(BF16)
    before = jnp.dot(lower, chosen.astype(BF16), preferred_element_type=F32) + carry[...]
    idx_o = jnp.zeros((t, ep), I32)
    gate_o = jnp.zeros((t, ep), F32)
    rank_o = jnp.zeros((t, ep), I32)
    for k in range(TOP_K):
        rank_k = jnp.sum(jnp.where(lane == idxs[k], before, 0.0), axis=-1, keepdims=True)
        idx_o = jnp.where(lane == k, idxs[k], idx_o)
        gate_o = jnp.where(lane == k, exps[k] / denom, gate_o)
        rank_o = jnp.where(lane == k, rank_k.astype(I32), rank_o)
    idx_ref[...] = idx_o[:, :TOP_K]
    gate_ref[...] = gate_o[:, :TOP_K]
    rank_ref[...] = rank_o[:, :TOP_K]
    carry[...] += jnp.sum(chosen, axis=0, keepdims=True)
    cnt_ref[...] = carry[...].astype(I32)


def _router(h, w_pad, b_pad):
    n, d = h.shape
    ep = w_pad.shape[1]
    t = _pick(n, (256, 128, 64, 32))
    small = pl.BlockSpec((t, TOP_K), lambda i: (i, 0))
    return pl.pallas_call(
        _router_kernel,
        out_shape=(jax.ShapeDtypeStruct((n, TOP_K), I32), jax.ShapeDtypeStruct((n, TOP_K), F32),
                   jax.ShapeDtypeStruct((n, TOP_K), I32), jax.ShapeDtypeStruct((1, ep), I32)),
        grid=(n // t,),
        in_specs=[pl.BlockSpec((t, d), lambda i: (i, 0)),
                  pl.BlockSpec((d, ep), lambda i: (0, 0)),
                  pl.BlockSpec((1, ep), lambda i: (0, 0))],
        out_specs=(small, small, small, pl.BlockSpec((1, ep), lambda i: (0, 0))),
        scratch_shapes=[pltpu.VMEM((1, ep), F32)],
        compiler_params=_cparams(("arbitrary",)),
        name="moe_router",
    )(h, w_pad, b_pad)


def _dispatch_kernel(pstart_ref, pad_ref, h_ref, idx_hbm, rank_hbm, xs_hbm,
                     idx_s, rank_s, rows, zblk, sem_m, sem_r, *, n_exp):
    i = pl.program_id(0)
    td, d = h_ref.shape
    na = TOP_K * td
    bm = zblk.shape[0]
    rows[...] = h_ref[...].reshape(td, 1, d)

    @pl.when(i == 0)
    def _():
        zblk[...] = jnp.zeros_like(zblk)
        total = pstart_ref[n_exp - 1] + pad_ref[n_exp - 1]
        n_tail = (xs_hbm.shape[0] - total) // bm

        def zero_copy(start):
            return pltpu.make_async_copy(zblk, xs_hbm.at[pl.ds(start, bm)], sem_r)

        def fill(e, _):
            @pl.when(pad_ref[e] > 0)
            def _():
                zero_copy(pstart_ref[e] + pad_ref[e] - bm).start()
            return 0

        def drain(e, _):
            @pl.when(pad_ref[e] > 0)
            def _():
                zero_copy(0).wait()
            return 0

        lax.fori_loop(0, n_exp, fill, 0)
        lax.fori_loop(0, n_tail, lambda j, c: (zero_copy(total + j * bm).start(), c)[1], 0)
        lax.fori_loop(0, n_exp, drain, 0)
        lax.fori_loop(0, n_tail, lambda j, c: (zero_copy(0).wait(), c)[1], 0)

    ci = pltpu.make_async_copy(idx_hbm.at[pl.ds(i * na, na)], idx_s, sem_m.at[0])
    cr = pltpu.make_async_copy(rank_hbm.at[pl.ds(i * na, na)], rank_s, sem_m.at[1])
    ci.start()
    cr.start()
    ci.wait()
    cr.wait()

    def row(t, _):
        for k in range(TOP_K):
            j = TOP_K * t + k
            dst = pstart_ref[idx_s[j]] + rank_s[j]
            pltpu.make_async_copy(rows.at[t], xs_hbm.at[dst], sem_r).start()
        return 0

    lax.fori_loop(0, td, row, 0)
    pltpu.make_async_copy(xs_hbm.at[pl.ds(0, na)], xs_hbm.at[pl.ds(0, na)], sem_r).wait()


def _dispatch(h, idx_flat, rank_flat, pstart, padded, cap, bm):
    n, d = h.shape
    n_exp = pstart.shape[0]
    td = _pick(n, (256, 128, 64, 32))
    grid_spec = pltpu.PrefetchScalarGridSpec(
        num_scalar_prefetch=2,
        grid=(n // td,),
        in_specs=[pl.BlockSpec((td, d), lambda i, ps, ct: (i, 0)),
                  pl.BlockSpec(memory_space=pl.ANY),
                  pl.BlockSpec(memory_space=pl.ANY)],
        out_specs=pl.BlockSpec(memory_space=pl.ANY),
        scratch_shapes=[pltpu.SMEM((TOP_K * td,), I32), pltpu.SMEM((TOP_K * td,), I32),
                        pltpu.VMEM((td, 1, d), F32), pltpu.VMEM((bm, 1, d), F32),
                        pltpu.SemaphoreType.DMA((2,)), pltpu.SemaphoreType.DMA],
    )
    return pl.pallas_call(
        functools.partial(_dispatch_kernel, n_exp=n_exp),
        out_shape=jax.ShapeDtypeStruct((cap, 1, d), F32),
        grid_spec=grid_spec,
        compiler_params=_cparams(("arbitrary",)),
        name="moe_dispatch",
    )(pstart, padded, h, idx_flat, rank_flat)


def _experts_kernel(be_ref, nvb_ref, xs_ref, w1_ref, b1_ref, w2_ref, b2_ref, ys_ref, w1b, w2b, x2d):
    i = pl.program_id(0)
    valid = i < nvb_ref[0]
    f, d = w2b.shape
    bm = x2d.shape[0]
    fresh = jnp.logical_or(i == 0, be_ref[i] != be_ref[jnp.maximum(i - 1, 0)])

    @pl.when(jnp.logical_and(valid, fresh))
    def _():
        w1b[...] = w1_ref[0].astype(BF16)
        w2b[...] = w2_ref[0].astype(BF16)

    @pl.when(valid)
    def _():
        x2d[...] = xs_ref[...].reshape(bm, d)
        h = jnp.dot(x2d[...].astype(BF16), w1b[...], preferred_element_type=F32) + b1_ref[0]
        gate = jnp.minimum(h[:, :f], SWIGLU_LIMIT)
        up = jnp.clip(h[:, f:], -SWIGLU_LIMIT, SWIGLU_LIMIT)
        act = (up + 1.0) * gate * jax.nn.sigmoid(SWIGLU_ALPHA * gate)
        y = jnp.dot(act.astype(BF16), w2b[...], preferred_element_type=F32) + b2_ref[0]
        ys_ref[...] = y.reshape(bm, 1, d)

    @pl.when(jnp.logical_not(valid))
    def _():
        ys_ref[...] = jnp.zeros_like(ys_ref)


def _experts(xs, block_e, n_valid, w1, b1, w2, b2, n_blocks, bm):
    d = xs.shape[-1]
    e, _, f2 = w1.shape
    f = f2 // 2
    grid_spec = pltpu.PrefetchScalarGridSpec(
        num_scalar_prefetch=2,
        grid=(n_blocks,),
        in_specs=[pl.BlockSpec((bm, 1, d), lambda i, be, nv: (jnp.minimum(i, nv[0] - 1), 0, 0)),
                  pl.BlockSpec((1, d, f2), lambda i, be, nv: (be[i], 0, 0)),
                  pl.BlockSpec((1, 1, f2), lambda i, be, nv: (be[i], 0, 0)),
                  pl.BlockSpec((1, f, d), lambda i, be, nv: (be[i], 0, 0)),
                  pl.BlockSpec((1, 1, d), lambda i, be, nv: (be[i], 0, 0))],
        out_specs=pl.BlockSpec((bm, 1, d), lambda i, be, nv: (i, 0, 0)),
        scratch_shapes=[pltpu.VMEM((d, f2), BF16), pltpu.VMEM((f, d), BF16), pltpu.VMEM((bm, d), F32)],
    )
    return pl.pallas_call(
        _experts_kernel,
        out_shape=jax.ShapeDtypeStruct((n_blocks * bm, 1, d), F32),
        grid_spec=grid_spec,
        compiler_params=_cparams(("arbitrary",)),
        name="moe_experts",
    )(block_e, n_valid, xs, w1, b1.reshape(e, 1, f2), w2, b2.reshape(e, 1, d))


def _combine_kernel(pstart_ref, h_ref, gate_ref, idx_hbm, rank_hbm, ys_hbm, g_ref, b_ref, o_ref,
                    idx_s, rank_s, buf, y2d, sem_m, sem_r, *, alpha, blk_off):
    i = pl.program_id(0)
    tf, d = h_ref.shape
    na = TOP_K * tf
    ci = pltpu.make_async_copy(idx_hbm.at[pl.ds((i + blk_off) * na, na)], idx_s, sem_m.at[0])
    cr = pltpu.make_async_copy(rank_hbm.at[pl.ds((i + blk_off) * na, na)], rank_s, sem_m.at[1])
    ci.start()
    cr.start()
    ci.wait()
    cr.wait()

    def row(t, _):
        for k in range(TOP_K):
            j = TOP_K * t + k
            src = pstart_ref[idx_s[j]] + rank_s[j]
            pltpu.make_async_copy(ys_hbm.at[src], buf.at[k * tf + t], sem_r).start()
        return 0

    lax.fori_loop(0, tf, row, 0)
    pltpu.make_async_copy(buf, buf, sem_r).wait()
    y = alpha * h_ref[...]
    gates = gate_ref[...]
    for k in range(TOP_K):
        y2d[...] = buf[pl.ds(k * tf, tf)].reshape(tf, d)
        y = y + gates[:, k:k + 1] * y2d[...]
    o_ref[...] = _layer_norm(y, g_ref[...], b_ref[...])


def _combine(h, gates, idx_flat, rank_flat, ys, pstart, g, beta, alpha, n, row_off):
    d = h.shape[1]
    tf = _pick(math.gcd(n, row_off) if row_off else n, (256, 128, 64, 32))
    bo = row_off // tf
    grid_spec = pltpu.PrefetchScalarGridSpec(
        num_scalar_prefetch=1,
        grid=(n // tf,),
        in_specs=[pl.BlockSpec((tf, d), lambda i, ps: (i + bo, 0)),
                  pl.BlockSpec((tf, TOP_K), lambda i, ps: (i + bo, 0)),
                  pl.BlockSpec(memory_space=pl.ANY),
                  pl.BlockSpec(memory_space=pl.ANY),
                  pl.BlockSpec(memory_space=pl.ANY),
                  pl.BlockSpec((1, d), lambda i, ps: (0, 0)),
                  pl.BlockSpec((1, d), lambda i, ps: (0, 0))],
        out_specs=pl.BlockSpec((tf, d), lambda i, ps: (i, 0)),
        scratch_shapes=[pltpu.SMEM((TOP_K * tf,), I32), pltpu.SMEM((TOP_K * tf,), I32),
                        pltpu.VMEM((TOP_K * tf, 1, d), F32), pltpu.VMEM((tf, d), F32),
                        pltpu.SemaphoreType.DMA((2,)), pltpu.SemaphoreType.DMA],
    )
    return pl.pallas_call(
        functools.partial(_combine_kernel, alpha=alpha, blk_off=bo),
        out_shape=jax.ShapeDtypeStruct((n, d), F32),
        grid_spec=grid_spec,
        compiler_params=_cparams(("arbitrary",)),
        name="moe_combine_ln",
    )(pstart, h, gates, idx_flat, rank_flat, ys, g.reshape(1, -1), beta.reshape(1, -1))


def _moe_ln(h, w_router, b_router, w1, b1, w2, b2, g, beta, alpha, splits):
    n, d = h.shape
    n_exp = w_router.shape[1]
    ep = -(-n_exp // LANES) * LANES
    bm = EXPERT_BLOCK
    w_pad = jnp.zeros((d, ep), F32).at[:, :n_exp].set(w_router)
    b_pad = jnp.full((1, ep), NEG_BIG, F32).at[0, :n_exp].set(b_router)
    idx, gates, rank, counts = _router(h, w_pad, b_pad)
    counts = counts[0, :n_exp]
    padded = (counts + bm - 1) // bm * bm
    ends = jnp.cumsum(padded)
    pstart = (ends - padded).astype(I32)
    n_blocks = -(-(n * TOP_K + n_exp * (bm - 1)) // bm)
    block_e = jnp.minimum(jnp.searchsorted(ends, jnp.arange(n_blocks, dtype=I32) * bm, side="right"),
                          n_exp - 1).astype(I32)
    n_valid = (ends[-1:] // bm).astype(I32)
    idx_flat, rank_flat = idx.reshape(-1), rank.reshape(-1)
    xs = _dispatch(h, idx_flat, rank_flat, pstart, padded.astype(I32), n_blocks * bm, bm)
    ys = _experts(xs, block_e, n_valid, w1, b1, w2, b2, n_blocks, bm)
    return [_combine(h, gates, idx_flat, rank_flat, ys, pstart, g, beta, alpha, rows, off)
            for off, rows in splits]


def kernel(x_prompt, x_sample, state_conv, cache_k, cache_v, page_table, w_pw1, b_pw1, dw_w, dw_b, cn_g, cn_b, w_pw2, b_pw2, w_qkv, b_qkv, w_o, ln_mix_g, ln_mix_b, ln_ffn_g, ln_ffn_b, w_router, b_router, w_e1, b_e1, w_e2, b_e2):
    b, s, d = x_prompt.shape
    db, t, _ = x_sample.shape
    depth = ln_mix_g.shape[0]
    n_heads, dh = cache_k.shape[3], cache_k.shape[4]
    width = dw_w.shape[1]
    n_p, n_s = b * s, db * t
    n = n_p + n_s
    alpha = (2 * depth) ** 0.25
    hist = -(-(width - 1) // SUBLANES) * SUBLANES

    xp, xs_ = x_prompt.reshape(n_p, d), x_sample.reshape(n_s, d)
    h = None
    conv_p, conv_s, k_p, v_p, k_s, v_s = [], [], [], [], [], []
    for i in range(depth):
        j = i // 2
        last = i == depth - 1
        if i % 2 == 0:
            w1_bf = w_pw1[j].astype(BF16)
            up = _mm_glu(xp if h is None else h[:n_p], w1_bf, b_pw1[j])
            us = _mm_glu(xs_ if h is None else h[n_p:], w1_bf, b_pw1[j])
            state = jnp.pad(state_conv[j].astype(F32), ((0, 0), (hist - (width - 1), 0), (0, 0)))
            yp, cp = _conv(up.reshape(b, s, d), None, dw_w[j], dw_b[j], cn_g[j], cn_b[j])
            ys, cs = _conv(us.reshape(db, t, d), state, dw_w[j], dw_b[j], cn_g[j], cn_b[j])
            conv_p.append(cp)
            conv_s.append(cs)
            w2_bf = w_pw2[j].astype(BF16)
            res = [(xp, 0), (xs_, 0)] if h is None else [(h, 0), (h, n_p)]
            segs = [(yp.reshape(n_p, d), *res[0]), (ys.reshape(n_s, d), *res[1])]
            hm = _mm_res_ln(segs, w2_bf, b_pw2[j], ln_mix_g[i], ln_mix_b[i], alpha)
        else:
            wq_bf = w_qkv[j].astype(BF16)
            qp, kp, vp = _mm_qkv(h, wq_bf, b_qkv[j], n_p, 0)
            qs, ks, vs = _mm_qkv(h, wq_bf, b_qkv[j], n_s, n_p)
            op = _attn_prompt(qp, kp, vp, b, s, dh)
            pool, page = cache_k.shape[1], cache_k.shape[2]
            os_ = _attn_sample(qs, ks, vs, cache_k[j].reshape(pool, page, d).astype(F32),
                               cache_v[j].reshape(pool, page, d).astype(F32), page_table, dh)
            k_p.append(kp.reshape(b, s, n_heads, dh))
            v_p.append(vp.reshape(b, s, n_heads, dh))
            k_s.append(ks.reshape(db, t, n_heads, dh))
            v_s.append(vs.reshape(db, t, n_heads, dh))
            hm = _mm_res_ln([(op, h, 0), (os_, h, n_p)], w_o[j].astype(BF16), jnp.zeros((d,), F32),
                            ln_mix_g[i], ln_mix_b[i], alpha)
        splits = [(0, n_p), (n_p, n_s)] if last else [(0, n)]
        outs = _moe_ln(hm, w_router[i], b_router[i], w_e1[i], b_e1[i], w_e2[i], b_e2[i],
                       ln_ffn_g[i], ln_ffn_b[i], alpha, splits)
        if last:
            y_p, y_s = outs
        else:
            h = outs[0]
    return (y_p.reshape(b, s, d), y_s.reshape(db, t, d), jnp.stack(conv_p), jnp.stack(conv_s),
            jnp.stack(k_p), jnp.stack(v_p), jnp.stack(k_s), jnp.stack(v_s))
```

```python
import functools
import math

import jax
import jax.numpy as jnp
from jax import lax
from jax.experimental import pallas as pl
from jax.experimental.pallas import tpu as pltpu

F32 = jnp.float32
BF16 = jnp.bfloat16
I32 = jnp.int32

TOP_K = 4
SWIGLU_LIMIT = 7.0
SWIGLU_ALPHA = 1.702
LN_EPS = 1e-5
LANES = 128
SUBLANES = 8
VMEM_LIMIT = 56 * 1024 * 1024
EXPERT_BLOCK = 256
NEG_BIG = -1e30
LOG2_E = math.log2(math.e)


def _pick(n, cands):
    for c in cands:
        if n % c == 0:
            return c
    raise ValueError(f"no tile in {cands} divides {n}")


def _cparams(sem):
    return pltpu.CompilerParams(dimension_semantics=sem, vmem_limit_bytes=VMEM_LIMIT)


def _layer_norm(x, g, b):
    mu = jnp.mean(x, axis=-1, keepdims=True)
    xc = x - mu
    var = jnp.mean(xc * xc, axis=-1, keepdims=True)
    return xc * lax.rsqrt(var + LN_EPS) * g + b


def _mm_glu_kernel(x_ref, w_ref, b_ref, u_ref):
    d = u_ref.shape[-1]
    h = jnp.dot(x_ref[...].astype(BF16), w_ref[...], preferred_element_type=F32) + b_ref[...]
    u_ref[...] = h[:, :d] * jax.nn.sigmoid(h[:, d:])


def _mm_glu(x, w_bf, b):
    n, d = x.shape
    tm = _pick(n, (512, 256, 128, 64, 32, 16, 8))
    return pl.pallas_call(
        _mm_glu_kernel,
        out_shape=jax.ShapeDtypeStruct((n, d), F32),
        grid=(n // tm,),
        in_specs=[pl.BlockSpec((tm, d), lambda i: (i, 0)),
                  pl.BlockSpec((d, 2 * d), lambda i: (0, 0)),
                  pl.BlockSpec((1, 2 * d), lambda i: (0, 0))],
        out_specs=pl.BlockSpec((tm, d), lambda i: (i, 0)),
        compiler_params=_cparams(("arbitrary",)),
        name="mm_glu",
    )(x, w_bf, b.reshape(1, -1))


def _mm_res_ln_kernel(w_ref, b_ref, g_ref, beta_ref, *rest, alpha, bounds):
    o_ref = rest[-1]
    i = pl.program_id(0)
    for s, (lo, hi) in enumerate(bounds):
        a_ref, res_ref = rest[2 * s], rest[2 * s + 1]

        @pl.when(jnp.logical_and(i >= lo, i < hi))
        def _(a_ref=a_ref, res_ref=res_ref):
            m = jnp.dot(a_ref[...].astype(BF16), w_ref[...], preferred_element_type=F32) + b_ref[...]
            o_ref[...] = _layer_norm(alpha * res_ref[...] + m, g_ref[...], beta_ref[...])


def _mm_res_ln(segs, w_bf, b, g, beta, alpha):
    k, d = w_bf.shape
    tm = functools.reduce(math.gcd, [x for a, _, off in segs for x in (a.shape[0], off)])
    tm = _pick(tm, (512, 256, 128, 64, 32, 16, 8))
    vec = pl.BlockSpec((1, d), lambda i: (0, 0))
    in_specs = [pl.BlockSpec((k, d), lambda i: (0, 0)), vec, vec, vec]
    args = [w_bf, b.reshape(1, -1), g.reshape(1, -1), beta.reshape(1, -1)]
    bounds, lo = [], 0
    for a, res, off in segs:
        nt = a.shape[0] // tm
        in_specs.append(pl.BlockSpec((tm, k), lambda i, lo=lo, nt=nt: (jnp.clip(i - lo, 0, nt - 1), 0)))
        in_specs.append(pl.BlockSpec((tm, d), lambda i, lo=lo, nt=nt, so=off // tm:
                                     (jnp.clip(i - lo, 0, nt - 1) + so, 0)))
        args += [a, res]
        bounds.append((lo, lo + nt))
        lo += nt
    return pl.pallas_call(
        functools.partial(_mm_res_ln_kernel, alpha=alpha, bounds=tuple(bounds)),
        out_shape=jax.ShapeDtypeStruct((lo * tm, d), F32),
        grid=(lo,),
        in_specs=in_specs,
        out_specs=pl.BlockSpec((tm, d), lambda i: (i, 0)),
        compiler_params=_cparams(("arbitrary",)),
        name="mm_res_ln",
    )(*args)


def _mm_qkv_kernel(h_ref, w_ref, b_ref, q_ref, k_ref, v_ref):
    d = k_ref.shape[-1]
    r = jnp.dot(h_ref[...].astype(BF16), w_ref[...], preferred_element_type=F32) + b_ref[...]
    q_ref[...] = r[:, :d].astype(BF16)
    k_ref[...] = r[:, d:2 * d]
    v_ref[...] = r[:, 2 * d:]


def _mm_qkv(h, w_bf, b, n, row_off):
    d = h.shape[1]
    tm = _pick(math.gcd(n, row_off) if row_off else n, (512, 256, 128, 64, 32, 16, 8))
    ro = row_off // tm
    return pl.pallas_call(
        _mm_qkv_kernel,
        out_shape=(jax.ShapeDtypeStruct((n, d), BF16),
                   jax.ShapeDtypeStruct((n, d), F32),
                   jax.ShapeDtypeStruct((n, d), F32)),
        grid=(n // tm,),
        in_specs=[pl.BlockSpec((tm, d), lambda i: (i + ro, 0)),
                  pl.BlockSpec((d, 3 * d), lambda i: (0, 0)),
                  pl.BlockSpec((1, 3 * d), lambda i: (0, 0))],
        out_specs=(pl.BlockSpec((tm, d), lambda i: (i, 0)),) * 3,
        compiler_params=_cparams(("arbitrary",)),
        name="mm_qkv",
    )(h, w_bf, b.reshape(1, -1))


def _conv_kernel(*refs, width, hist, has_state, n_tiles):
    if has_state:
        u_ref, st_ref, w_ref, wb_ref, g_ref, b_ref, y_ref, cs_ref, buf = refs
    else:
        u_ref, w_ref, wb_ref, g_ref, b_ref, y_ref, cs_ref, buf = refs
    t = u_ref.shape[1]
    ti = pl.program_id(1)

    @pl.when(ti == 0)
    def _():
        if has_state:
            buf[0:hist, :] = st_ref[0]
        else:
            buf[0:hist, :] = jnp.zeros((hist, buf.shape[1]), F32)

    buf[hist:hist + t, :] = u_ref[0]
    lo = hist - (width - 1)
    acc = jnp.zeros((t, buf.shape[1]), F32) + wb_ref[...]
    for k in range(width):
        acc = acc + buf[lo + k:lo + k + t, :] * w_ref[k:k + 1, :]
    y = _layer_norm(acc, g_ref[...], b_ref[...])
    y_ref[0] = (y * jax.nn.sigmoid(y)).astype(BF16)
    cs_ref[0] = buf[t + lo:t + hist, :]
    if n_tiles > 1:
        buf[0:hist, :] = buf[t:t + hist, :]


def _conv(u, state, dw_w, dw_b, cn_g, cn_b):
    b, s, d = u.shape
    width = dw_w.shape[0]
    hist = -(-(width - 1) // SUBLANES) * SUBLANES
    t = _pick(s, (512, 256, 128, 64, 32, 16, 8))
    n_tiles = s // t
    assert n_tiles == 1 or t >= hist
    vec = pl.BlockSpec((1, d), lambda i, j: (0, 0))
    in_specs = [pl.BlockSpec((1, t, d), lambda i, j: (i, j, 0))]
    args = [u]
    if state is not None:
        in_specs.append(pl.BlockSpec((1, hist, d), lambda i, j: (i, 0, 0)))
        args.append(state)
    in_specs += [pl.BlockSpec((width, d), lambda i, j: (0, 0)), vec, vec, vec]
    args += [dw_w, dw_b.reshape(1, -1), cn_g.reshape(1, -1), cn_b.reshape(1, -1)]
    return pl.pallas_call(
        functools.partial(_conv_kernel, width=width, hist=hist, has_state=state is not None,
                          n_tiles=n_tiles),
        out_shape=(jax.ShapeDtypeStruct((b, s, d), BF16),
                   jax.ShapeDtypeStruct((b, width - 1, d), F32)),
        grid=(b, n_tiles),
        in_specs=in_specs,
        out_specs=(pl.BlockSpec((1, t, d), lambda i, j: (i, j, 0)),
                   pl.BlockSpec((1, width - 1, d), lambda i, j: (i, 0, 0))),
        scratch_shapes=[pltpu.VMEM((hist + t, d), F32)],
        compiler_params=_cparams(("arbitrary", "arbitrary")),
        name="conv_ln_swish",
    )(*args)


def _sb_block(z, mask, upper, run):
    neg_abs = lax.bitcast_convert_type(lax.bitcast_convert_type(z, jnp.uint32) | jnp.uint32(0x80000000), F32)
    drop = jnp.maximum(z, 0.0) + jnp.log2(1.0 + jnp.exp2(neg_abs))
    log_beta = z - drop
    if mask is not None:
        drop = jnp.where(mask, drop, 0.0)
    later = jnp.dot(drop.astype(BF16), upper, preferred_element_type=F32)
    a = jnp.exp2(log_beta - later - run)
    if mask is not None:
        a = jnp.where(mask, a, 0.0)
    return a.astype(BF16), run + jnp.sum(drop, axis=1, keepdims=True)


def _attn_prompt_kernel(q_ref, k_ref, v_ref, o_ref, kb, vb, *, scale, dh):
    qi = pl.program_id(2)
    blk, width = q_ref.shape
    heads = width // dh

    @pl.when(qi == 0)
    def _():
        kb[...] = k_ref[...].astype(BF16)
        vb[...] = v_ref[...].astype(BF16)

    q = q_ref[...]
    lane = lax.broadcasted_iota(I32, (blk, width), 1)
    row = lax.broadcasted_iota(I32, (blk, blk), 0)
    col = lax.broadcasted_iota(I32, (blk, blk), 1)
    upper = (row > col).astype(BF16)
    causal = col < row
    q_heads = [jnp.where((lane >= h * dh) & (lane < (h + 1) * dh), q, jnp.zeros_like(q))
               for h in range(heads)]

    def block(j, mask, carry):
        off = pl.multiple_of(j * blk, blk)
        ks = kb[pl.ds(off, blk), :]
        vs = vb[pl.ds(off, blk), :]
        out = []
        for h in range(heads):
            acc, run = carry[h]
            z = lax.dot_general(q_heads[h], ks, (((1,), (1,)), ((), ())),
                                preferred_element_type=F32) * scale
            a, run = _sb_block(z, mask, upper, run)
            out.append((acc + jnp.dot(a, vs, preferred_element_type=F32), run))
        return tuple(out)

    init = tuple((jnp.zeros((blk, width), F32), jnp.zeros((blk, 1), F32)) for _ in range(heads))
    carry = block(qi, causal, init)
    carry = lax.fori_loop(0, qi, lambda jj, c: block(qi - 1 - jj, None, c), carry)
    out = carry[0][0]
    for h in range(1, heads):
        out = jnp.where(lane >= h * dh, carry[h][0], out)
    o_ref[...] = out.astype(BF16)


def _attn_prompt(q, k, v, batch, seq, dh):
    n, d = q.shape
    width = LANES
    assert width % dh == 0 and d % width == 0
    blk = _pick(seq, (512, 256, 128))
    nq = seq // blk
    return pl.pallas_call(
        functools.partial(_attn_prompt_kernel, scale=dh ** -0.5 * LOG2_E, dh=dh),
        out_shape=jax.ShapeDtypeStruct((n, d), BF16),
        grid=(batch, d // width, nq),
        in_specs=[pl.BlockSpec((blk, width), lambda b, h, i: (b * nq + i, h)),
                  pl.BlockSpec((seq, width), lambda b, h, i: (b, h)),
                  pl.BlockSpec((seq, width), lambda b, h, i: (b, h))],
        out_specs=pl.BlockSpec((blk, width), lambda b, h, i: (b * nq + i, h)),
        scratch_shapes=[pltpu.VMEM((seq, width), BF16), pltpu.VMEM((seq, width), BF16)],
        compiler_params=_cparams(("arbitrary", "arbitrary", "arbitrary")),
        name="attn_prompt",
    )(q, k, v)


def _attn_sample_kernel(pt_ref, q_ref, kn_ref, vn_ref, *rest, scale, dh, n_heads, ppb):
    del pt_ref
    k_refs, v_refs = rest[:ppb], rest[ppb:2 * ppb]
    o_ref, acc, run, qx = rest[2 * ppb:]
    p = pl.program_id(1)
    t, d = q_ref.shape
    page = k_refs[0].shape[2]
    m = n_heads * t
    rowh = lax.broadcasted_iota(I32, (m, d), 0) // t
    colh = lax.broadcasted_iota(I32, (m, d), 1) // dh
    own = rowh == colh

    def upper_of(n):
        return (lax.broadcasted_iota(I32, (n, n), 0) > lax.broadcasted_iota(I32, (n, n), 1)).astype(BF16)

    def step(kc, vc, mask):
        z = lax.dot_general(qx[...], kc, (((1,), (1,)), ((), ())), preferred_element_type=F32) * scale
        a, new_run = _sb_block(z, mask, upper_of(kc.shape[0]), run[...])
        acc[...] += jnp.dot(a, vc, preferred_element_type=F32)
        run[...] = new_run

    @pl.when(p == 0)
    def _():
        qt = jnp.concatenate([q_ref[...]] * n_heads, axis=0)
        qx[...] = jnp.where(own, qt, jnp.zeros_like(qt))
        acc[...] = jnp.zeros_like(acc)
        run[...] = jnp.zeros_like(run)
        pad = jnp.zeros((page - t, d), BF16)
        kc = jnp.concatenate([kn_ref[...].astype(BF16), pad], axis=0)
        vc = jnp.concatenate([vn_ref[...].astype(BF16), pad], axis=0)
        key = lax.broadcasted_iota(I32, (m, page), 1)
        qpos = lax.broadcasted_iota(I32, (m, page), 0) % t
        step(kc, vc, key < qpos)

    def dense(ref):
        return ref[0, 0].reshape(page, d).astype(BF16)

    kc = jnp.concatenate([dense(k_refs[r]) for r in reversed(range(ppb))], axis=0)
    vc = jnp.concatenate([dense(v_refs[r]) for r in reversed(range(ppb))], axis=0)
    step(kc, vc, None)

    @pl.when(p == pl.num_programs(1) - 1)
    def _():
        sel = jnp.where(own, acc[...], 0.0).reshape(n_heads, t, d)
        o_ref[...] = jnp.sum(sel, axis=0).astype(BF16)


def _attn_sample(q, k_new, v_new, cache_k, cache_v, layer, page_table):
    n, d = q.shape
    db, n_pages = page_table.shape
    t = n // db
    _, _, page, n_heads, dh = cache_k.shape
    ppb = _pick(n_pages, (2, 1))
    steps = n_pages // ppb
    assert t <= page and n_heads * dh == d

    def page_spec(r):
        return pl.BlockSpec((1, 1, page, n_heads, dh),
                            lambda b, p, pt: (layer, pt[b, n_pages - 1 - (p * ppb + r)], 0, 0, 0))

    tok = pl.BlockSpec((t, d), lambda b, p, pt: (b, 0))
    grid_spec = pltpu.PrefetchScalarGridSpec(
        num_scalar_prefetch=1,
        grid=(db, steps),
        in_specs=[tok, tok, tok] + [page_spec(r) for r in range(ppb)] * 2,
        out_specs=tok,
        scratch_shapes=[pltpu.VMEM((n_heads * t, d), F32), pltpu.VMEM((n_heads * t, 1), F32),
                        pltpu.VMEM((n_heads * t, d), BF16)],
    )
    return pl.pallas_call(
        functools.partial(_attn_sample_kernel, scale=dh ** -0.5 * LOG2_E, dh=dh, n_heads=n_heads, ppb=ppb),
        out_shape=jax.ShapeDtypeStruct((n, d), BF16),
        grid_spec=grid_spec,
        compiler_params=_cparams(("arbitrary", "arbitrary")),
        name="attn_sample",
    )(page_table, q, k_new, v_new, *([cache_k] * ppb), *([cache_v] * ppb))


def _router_kernel(h_ref, w_ref, b_ref, idx_ref, gate_ref, rank_ref, cnt_ref, carry):
    i = pl.program_id(0)

    @pl.when(i == 0)
    def _():
        carry[...] = jnp.zeros_like(carry)

    logits = jnp.dot(h_ref[...], w_ref[...], preferred_element_type=F32,
                     precision=lax.Precision.HIGHEST) + b_ref[...]
    t, ep = logits.shape
    lane = lax.broadcasted_iota(I32, (t, ep), 1)
    left = logits
    chosen = jnp.zeros((t, ep), F32)
    vals, idxs = [], []
    for _ in range(TOP_K):
        top = jnp.max(left, axis=-1, keepdims=True)
        idx = jnp.min(jnp.where(left == top, lane, ep), axis=-1, keepdims=True)
        sel = lane == idx
        vals.append(top)
        idxs.append(idx)
        chosen = jnp.where(sel, 1.0, chosen)
        left = jnp.where(sel, -jnp.inf, left)
    exps = [jnp.exp(v - vals[0]) for v in vals]
    denom = exps[0]
    for e in exps[1:]:
        denom = denom + e
    lower = (lax.broadcasted_iota(I32, (t, t), 0) > lax.broadcasted_iota(I32, (t, t), 1)).astype(BF16)
    before = jnp.dot(lower, chosen.astype(BF16), preferred_element_type=F32) + carry[...]
    idx_o = jnp.zeros((t, ep), I32)
    gate_o = jnp.zeros((t, ep), F32)
    rank_o = jnp.zeros((t, ep), I32)
    for k in range(TOP_K):
        rank_k = jnp.sum(jnp.where(lane == idxs[k], before, 0.0), axis=-1, keepdims=True)
        idx_o = jnp.where(lane == k, idxs[k], idx_o)
        gate_o = jnp.where(lane == k, exps[k] / denom, gate_o)
        rank_o = jnp.where(lane == k, rank_k.astype(I32), rank_o)
    idx_ref[...] = idx_o[:, :TOP_K]
    gate_ref[...] = gate_o[:, :TOP_K]
    rank_ref[...] = rank_o[:, :TOP_K]
    carry[...] += jnp.sum(chosen, axis=0, keepdims=True)
    cnt_ref[...] = carry[...].astype(I32)


def _router(h, w_pad, b_pad):
    n, d = h.shape
    ep = w_pad.shape[1]
    t = _pick(n, (256, 128, 64, 32))
    small = pl.BlockSpec((t, TOP_K), lambda i: (i, 0))
    return pl.pallas_call(
        _router_kernel,
        out_shape=(jax.ShapeDtypeStruct((n, TOP_K), I32), jax.ShapeDtypeStruct((n, TOP_K), F32),
                   jax.ShapeDtypeStruct((n, TOP_K), I32), jax.ShapeDtypeStruct((1, ep), I32)),
        grid=(n // t,),
        in_specs=[pl.BlockSpec((t, d), lambda i: (i, 0)),
                  pl.BlockSpec((d, ep), lambda i: (0, 0)),
                  pl.BlockSpec((1, ep), lambda i: (0, 0))],
        out_specs=(small, small, small, pl.BlockSpec((1, ep), lambda i: (0, 0))),
        scratch_shapes=[pltpu.VMEM((1, ep), F32)],
        compiler_params=_cparams(("arbitrary",)),
        name="moe_router",
    )(h, w_pad, b_pad)


def _dispatch_kernel(pstart_ref, pad_ref, h_ref, idx_hbm, rank_hbm, xs_hbm,
                     idx_s0, idx_s1, rank_s0, rank_s1, rows0, rows1, zblk, sem_m, sem_r, sem_z, *, n_exp):
    i = pl.program_id(0)
    n_steps = pl.num_programs(0)
    td, d = h_ref.shape
    na = TOP_K * td
    bm = zblk.shape[0]
    idx_s, rank_s, rows = (idx_s0, idx_s1), (rank_s0, rank_s1), (rows0, rows1)

    def meta(step, slot):
        return (pltpu.make_async_copy(idx_hbm.at[pl.ds(step * td, td)], idx_s[slot], sem_m.at[0, slot]),
                pltpu.make_async_copy(rank_hbm.at[pl.ds(step * td, td)], rank_s[slot], sem_m.at[1, slot]))

    def wait_rows(slot):
        pltpu.make_async_copy(xs_hbm.at[pl.ds(0, na)], xs_hbm.at[pl.ds(0, na)], sem_r.at[slot]).wait()

    @pl.when(i == 0)
    def _():
        zblk[...] = jnp.zeros_like(zblk)
        total = pstart_ref[n_exp - 1] + pad_ref[n_exp - 1]
        n_tail = (xs_hbm.shape[0] - total) // bm

        def zero_copy(start):
            return pltpu.make_async_copy(zblk, xs_hbm.at[pl.ds(start, bm)], sem_z)

        def fill(e, _):
            @pl.when(pad_ref[e] > 0)
            def _():
                zero_copy(pstart_ref[e] + pad_ref[e] - bm).start()
            return 0

        def drain(e, _):
            @pl.when(pad_ref[e] > 0)
            def _():
                zero_copy(0).wait()
            return 0

        lax.fori_loop(0, n_exp, fill, 0)
        lax.fori_loop(0, n_tail, lambda j, c: (zero_copy(total + j * bm).start(), c)[1], 0)
        lax.fori_loop(0, n_exp, drain, 0)
        lax.fori_loop(0, n_tail, lambda j, c: (zero_copy(0).wait(), c)[1], 0)
        for c in meta(0, 0):
            c.start()

    def step(slot):
        @pl.when(i + 1 < n_steps)
        def _():
            for c in meta(i + 1, 1 - slot):
                c.start()

        for c in meta(i, slot):
            c.wait()
        rows[slot][...] = h_ref[...].reshape(td, 1, d)

        def row(t, _):
            for k in range(TOP_K):
                dst = pstart_ref[idx_s[slot][t, k]] + rank_s[slot][t, k]
                pltpu.make_async_copy(rows[slot].at[t], xs_hbm.at[dst], sem_r.at[slot]).start(priority=k % 2)
            return 0

        lax.fori_loop(0, td, row, 0)

        @pl.when(i > 0)
        def _():
            wait_rows(1 - slot)

        @pl.when(i == n_steps - 1)
        def _():
            wait_rows(slot)

    for slot in range(2):
        pl.when(i % 2 == slot)(functools.partial(step, slot))


def _dispatch(h, idx, rank, pstart, padded, cap, bm):
    n, d = h.shape
    n_exp = pstart.shape[0]
    td = _pick(n, (256, 128, 64, 32))
    grid_spec = pltpu.PrefetchScalarGridSpec(
        num_scalar_prefetch=2,
        grid=(n // td,),
        in_specs=[pl.BlockSpec((td, d), lambda i, ps, ct: (i, 0)),
                  pl.BlockSpec(memory_space=pl.ANY),
                  pl.BlockSpec(memory_space=pl.ANY)],
        out_specs=pl.BlockSpec(memory_space=pl.ANY),
        scratch_shapes=[pltpu.SMEM((td, TOP_K), I32)] * 4
        + [pltpu.VMEM((td, 1, d), F32), pltpu.VMEM((td, 1, d), F32), pltpu.VMEM((bm, 1, d), F32),
           pltpu.SemaphoreType.DMA((2, 2)), pltpu.SemaphoreType.DMA((2,)), pltpu.SemaphoreType.DMA],
    )
    return pl.pallas_call(
        functools.partial(_dispatch_kernel, n_exp=n_exp),
        out_shape=jax.ShapeDtypeStruct((cap, 1, d), F32),
        grid_spec=grid_spec,
        compiler_params=_cparams(("arbitrary",)),
        name="moe_dispatch",
    )(pstart, padded, h, idx, rank)


def _experts_kernel(be_ref, nvb_ref, xs_ref, w1_ref, b1_ref, w2_ref, b2_ref, ys_ref, w1b, w2b, x2d):
    i = pl.program_id(0)
    valid = i < nvb_ref[0]
    f, d = w2b.shape
    bm = x2d.shape[0]
    fresh = jnp.logical_or(i == 0, be_ref[i] != be_ref[jnp.maximum(i - 1, 0)])

    @pl.when(jnp.logical_and(valid, fresh))
    def _():
        w1b[...] = w1_ref[0].astype(BF16)
        w2b[...] = w2_ref[0].astype(BF16)

    @pl.when(valid)
    def _():
        x2d[...] = xs_ref[...].reshape(bm, d)
        h = jnp.dot(x2d[...].astype(BF16), w1b[...], preferred_element_type=F32) + b1_ref[0]
        gate = jnp.minimum(h[:, :f], SWIGLU_LIMIT)
        up = jnp.clip(h[:, f:], -SWIGLU_LIMIT, SWIGLU_LIMIT)
        act = (up + 1.0) * gate * jax.nn.sigmoid(SWIGLU_ALPHA * gate)
        y = jnp.dot(act.astype(BF16), w2b[...], preferred_element_type=F32) + b2_ref[0]
        ys_ref[...] = y.reshape(bm, 1, d)

    @pl.when(jnp.logical_not(valid))
    def _():
        ys_ref[...] = jnp.zeros_like(ys_ref)


def _experts(xs, block_e, n_valid, w1, b1, w2, b2, n_blocks, bm):
    d = xs.shape[-1]
    e, _, f2 = w1.shape
    f = f2 // 2
    grid_spec = pltpu.PrefetchScalarGridSpec(
        num_scalar_prefetch=2,
        grid=(n_blocks,),
        in_specs=[pl.BlockSpec((bm, 1, d), lambda i, be, nv: (jnp.minimum(i, nv[0] - 1), 0, 0)),
                  pl.BlockSpec((1, d, f2), lambda i, be, nv: (be[i], 0, 0)),
                  pl.BlockSpec((1, 1, f2), lambda i, be, nv: (be[i], 0, 0)),
                  pl.BlockSpec((1, f, d), lambda i, be, nv: (be[i], 0, 0)),
                  pl.BlockSpec((1, 1, d), lambda i, be, nv: (be[i], 0, 0))],
        out_specs=pl.BlockSpec((bm, 1, d), lambda i, be, nv: (i, 0, 0)),
        scratch_shapes=[pltpu.VMEM((d, f2), BF16), pltpu.VMEM((f, d), BF16), pltpu.VMEM((bm, d), F32)],
    )
    return pl.pallas_call(
        _experts_kernel,
        out_shape=jax.ShapeDtypeStruct((n_blocks * bm, 1, d), F32),
        grid_spec=grid_spec,
        compiler_params=_cparams(("arbitrary",)),
        name="moe_experts",
    )(block_e, n_valid, xs, w1, b1.reshape(e, 1, f2), w2, b2.reshape(e, 1, d))


def _combine_kernel(pstart_ref, h_ref, gate_ref, idx_hbm, rank_hbm, ys_hbm, g_ref, b_ref, o_ref,
                    idx_s0, idx_s1, rank_s0, rank_s1, buf0, buf1, y2d, sem_m, sem_r, *, alpha, blk_off):
    i = pl.program_id(0)
    n_steps = pl.num_programs(0)
    tf, d = h_ref.shape
    idx_s, rank_s, buf = (idx_s0, idx_s1), (rank_s0, rank_s1), (buf0, buf1)

    def meta(step, slot):
        rows = pl.ds((step + blk_off) * tf, tf)
        return (pltpu.make_async_copy(idx_hbm.at[rows], idx_s[slot], sem_m.at[0, slot]),
                pltpu.make_async_copy(rank_hbm.at[rows], rank_s[slot], sem_m.at[1, slot]))

    def gather(slot):
        def row(t, _):
            for k in range(TOP_K):
                src = pstart_ref[idx_s[slot][t, k]] + rank_s[slot][t, k]
                pltpu.make_async_copy(ys_hbm.at[src], buf[slot].at[k * tf + t], sem_r.at[slot]).start(priority=k % 2)
            return 0

        lax.fori_loop(0, tf, row, 0)

    @pl.when(i == 0)
    def _():
        for c in meta(0, 0):
            c.start()
        for c in meta(0, 0):
            c.wait()
        gather(0)

        @pl.when(n_steps > 1)
        def _():
            for c in meta(1, 1):
                c.start()

    def step(slot):
        @pl.when(i + 1 < n_steps)
        def _():
            for c in meta(i + 1, 1 - slot):
                c.wait()
            gather(1 - slot)

        @pl.when(i + 2 < n_steps)
        def _():
            for c in meta(i + 2, slot):
                c.start()

        pltpu.make_async_copy(buf[slot], buf[slot], sem_r.at[slot]).wait()
        y = alpha * h_ref[...]
        gates = gate_ref[...]
        for k in range(TOP_K):
            y2d[...] = buf[slot][pl.ds(k * tf, tf)].reshape(tf, d)
            y = y + gates[:, k:k + 1] * y2d[...]
        o_ref[...] = _layer_norm(y, g_ref[...], b_ref[...])

    for slot in range(2):
        pl.when(i % 2 == slot)(functools.partial(step, slot))


def _combine(h, gates, idx, rank, ys, pstart, g, beta, alpha, n, row_off):
    d = h.shape[1]
    tf = _pick(math.gcd(n, row_off) if row_off else n, (256, 128, 64, 32))
    bo = row_off // tf
    grid_spec = pltpu.PrefetchScalarGridSpec(
        num_scalar_prefetch=1,
        grid=(n // tf,),
        in_specs=[pl.BlockSpec((tf, d), lambda i, ps: (i + bo, 0)),
                  pl.BlockSpec((tf, TOP_K), lambda i, ps: (i + bo, 0)),
                  pl.BlockSpec(memory_space=pl.ANY),
                  pl.BlockSpec(memory_space=pl.ANY),
                  pl.BlockSpec(memory_space=pl.ANY),
                  pl.BlockSpec((1, d), lambda i, ps: (0, 0)),
                  pl.BlockSpec((1, d), lambda i, ps: (0, 0))],
        out_specs=pl.BlockSpec((tf, d), lambda i, ps: (i, 0)),
        scratch_shapes=[pltpu.SMEM((tf, TOP_K), I32)] * 4
        + [pltpu.VMEM((TOP_K * tf, 1, d), F32), pltpu.VMEM((TOP_K * tf, 1, d), F32), pltpu.VMEM((tf, d), F32),
           pltpu.SemaphoreType.DMA((2, 2)), pltpu.SemaphoreType.DMA((2,))],
    )
    return pl.pallas_call(
        functools.partial(_combine_kernel, alpha=alpha, blk_off=bo),
        out_shape=jax.ShapeDtypeStruct((n, d), F32),
        grid_spec=grid_spec,
        compiler_params=_cparams(("arbitrary",)),
        name="moe_combine_ln",
    )(pstart, h, gates, idx, rank, ys, g.reshape(1, -1), beta.reshape(1, -1))


def _moe_ln(h, w_router, b_router, w1, b1, w2, b2, g, beta, alpha, splits):
    n, d = h.shape
    n_exp = w_router.shape[1]
    ep = -(-n_exp // LANES) * LANES
    bm = EXPERT_BLOCK
    w_pad = jnp.zeros((d, ep), F32).at[:, :n_exp].set(w_router)
    b_pad = jnp.full((1, ep), NEG_BIG, F32).at[0, :n_exp].set(b_router)
    idx, gates, rank, counts = _router(h, w_pad, b_pad)
    counts = counts[0, :n_exp]
    padded = (counts + bm - 1) // bm * bm
    ends = jnp.cumsum(padded)
    pstart = (ends - padded).astype(I32)
    n_blocks = -(-(n * TOP_K + n_exp * (bm - 1)) // bm)
    starts = jnp.arange(n_blocks, dtype=I32) * bm
    block_e = jnp.minimum(jnp.sum(starts[:, None] >= ends[None, :], axis=1), n_exp - 1).astype(I32)
    n_valid = (ends[-1:] // bm).astype(I32)
    xs = _dispatch(h, idx, rank, pstart, padded.astype(I32), n_blocks * bm, bm)
    ys = _experts(xs, block_e, n_valid, w1, b1, w2, b2, n_blocks, bm)
    return [_combine(h, gates, idx, rank, ys, pstart, g, beta, alpha, rows, off)
            for off, rows in splits]


def kernel(x_prompt, x_sample, state_conv, cache_k, cache_v, page_table, w_pw1, b_pw1, dw_w, dw_b, cn_g, cn_b, w_pw2, b_pw2, w_qkv, b_qkv, w_o, ln_mix_g, ln_mix_b, ln_ffn_g, ln_ffn_b, w_router, b_router, w_e1, b_e1, w_e2, b_e2):
    b, s, d = x_prompt.shape
    db, t, _ = x_sample.shape
    depth = ln_mix_g.shape[0]
    n_heads, dh = cache_k.shape[3], cache_k.shape[4]
    width = dw_w.shape[1]
    n_p, n_s = b * s, db * t
    n = n_p + n_s
    alpha = (2 * depth) ** 0.25
    hist = -(-(width - 1) // SUBLANES) * SUBLANES

    xp, xs_ = x_prompt.reshape(n_p, d), x_sample.reshape(n_s, d)
    h = None
    conv_p, conv_s, k_p, v_p, k_s, v_s = [], [], [], [], [], []
    for i in range(depth):
        j = i // 2
        last = i == depth - 1
        if i % 2 == 0:
            w1_bf = w_pw1[j].astype(BF16)
            up = _mm_glu(xp if h is None else h[:n_p], w1_bf, b_pw1[j])
            us = _mm_glu(xs_ if h is None else h[n_p:], w1_bf, b_pw1[j])
            state = jnp.pad(state_conv[j].astype(F32), ((0, 0), (hist - (width - 1), 0), (0, 0)))
            yp, cp = _conv(up.reshape(b, s, d), None, dw_w[j], dw_b[j], cn_g[j], cn_b[j])
            ys, cs = _conv(us.reshape(db, t, d), state, dw_w[j], dw_b[j], cn_g[j], cn_b[j])
            conv_p.append(cp)
            conv_s.append(cs)
            w2_bf = w_pw2[j].astype(BF16)
            res = [(xp, 0), (xs_, 0)] if h is None else [(h, 0), (h, n_p)]
            segs = [(yp.reshape(n_p, d), *res[0]), (ys.reshape(n_s, d), *res[1])]
            hm = _mm_res_ln(segs, w2_bf, b_pw2[j], ln_mix_g[i], ln_mix_b[i], alpha)
        else:
            wq_bf = w_qkv[j].astype(BF16)
            qp, kp, vp = _mm_qkv(h, wq_bf, b_qkv[j], n_p, 0)
            qs, ks, vs = _mm_qkv(h, wq_bf, b_qkv[j], n_s, n_p)
            op = _attn_prompt(qp, kp, vp, b, s, dh)
            os_ = _attn_sample(qs, ks, vs, cache_k, cache_v, j, page_table)
            k_p.append(kp.reshape(b, s, n_heads, dh))
            v_p.append(vp.reshape(b, s, n_heads, dh))
            k_s.append(ks.reshape(db, t, n_heads, dh))
            v_s.append(vs.reshape(db, t, n_heads, dh))
            hm = _mm_res_ln([(op, h, 0), (os_, h, n_p)], w_o[j].astype(BF16), jnp.zeros((d,), F32),
                            ln_mix_g[i], ln_mix_b[i], alpha)
        splits = [(0, n_p), (n_p, n_s)] if last else [(0, n)]
        outs = _moe_ln(hm, w_router[i], b_router[i], w_e1[i], b_e1[i], w_e2[i], b_e2[i],
                       ln_ffn_g[i], ln_ffn_b[i], alpha, splits)
        if last:
            y_p, y_s = outs
        else:
            h = outs[0]
    return (y_p.reshape(b, s, d), y_s.reshape(db, t, d), jnp.stack(conv_p), jnp.stack(conv_s),
            jnp.stack(k_p), jnp.stack(v_p), jnp.stack(k_s), jnp.stack(v_s))
```

```python
import functools
import math

import jax
import jax.numpy as jnp
from jax import lax
from jax.experimental import pallas as pl
from jax.experimental.pallas import tpu as pltpu

F32 = jnp.float32
BF16 = jnp.bfloat16
I32 = jnp.int32

TOP_K = 4
SWIGLU_LIMIT = 7.0
SWIGLU_ALPHA = 1.702
LN_EPS = 1e-5
LANES = 128
SUBLANES = 8
VMEM_LIMIT = 56 * 1024 * 1024
EXPERT_BLOCK = 256
NEG_BIG = -1e30
LOG2_E = math.log2(math.e)


def _pick(n, cands):
    for c in cands:
        if n % c == 0:
            return c
    raise ValueError(f"no tile in {cands} divides {n}")


def _cparams(sem):
    return pltpu.CompilerParams(dimension_semantics=sem, vmem_limit_bytes=VMEM_LIMIT)


def _layer_norm(x, g, b):
    mu = jnp.mean(x, axis=-1, keepdims=True)
    xc = x - mu
    var = jnp.mean(xc * xc, axis=-1, keepdims=True)
    return xc * lax.rsqrt(var + LN_EPS) * g + b


def _mm_glu_kernel(x_ref, w_ref, b_ref, u_ref):
    d = u_ref.shape[-1]
    h = jnp.dot(x_ref[...].astype(BF16), w_ref[...], preferred_element_type=F32) + b_ref[...]
    u_ref[...] = h[:, :d] * jax.nn.sigmoid(h[:, d:])


def _mm_glu(x, w_bf, b):
    n, d = x.shape
    tm = _pick(n, (512, 256, 128, 64, 32, 16, 8))
    return pl.pallas_call(
        _mm_glu_kernel,
        out_shape=jax.ShapeDtypeStruct((n, d), F32),
        grid=(n // tm,),
        in_specs=[pl.BlockSpec((tm, d), lambda i: (i, 0)),
                  pl.BlockSpec((d, 2 * d), lambda i: (0, 0)),
                  pl.BlockSpec((1, 2 * d), lambda i: (0, 0))],
        out_specs=pl.BlockSpec((tm, d), lambda i: (i, 0)),
        compiler_params=_cparams(("arbitrary",)),
        name="mm_glu",
    )(x, w_bf, b.reshape(1, -1))


def _mm_res_ln_kernel(w_ref, b_ref, g_ref, beta_ref, *rest, alpha, bounds):
    o_ref = rest[-1]
    i = pl.program_id(0)
    for s, (lo, hi) in enumerate(bounds):
        a_ref, res_ref = rest[2 * s], rest[2 * s + 1]

        @pl.when(jnp.logical_and(i >= lo, i < hi))
        def _(a_ref=a_ref, res_ref=res_ref):
            m = jnp.dot(a_ref[...].astype(BF16), w_ref[...], preferred_element_type=F32) + b_ref[...]
            o_ref[...] = _layer_norm(alpha * res_ref[...] + m, g_ref[...], beta_ref[...])


def _mm_res_ln(segs, w_bf, b, g, beta, alpha):
    k, d = w_bf.shape
    tm = functools.reduce(math.gcd, [x for a, _, off in segs for x in (a.shape[0], off)])
    tm = _pick(tm, (512, 256, 128, 64, 32, 16, 8))
    vec = pl.BlockSpec((1, d), lambda i: (0, 0))
    in_specs = [pl.BlockSpec((k, d), lambda i: (0, 0)), vec, vec, vec]
    args = [w_bf, b.reshape(1, -1), g.reshape(1, -1), beta.reshape(1, -1)]
    bounds, lo = [], 0
    for a, res, off in segs:
        nt = a.shape[0] // tm
        in_specs.append(pl.BlockSpec((tm, k), lambda i, lo=lo, nt=nt: (jnp.clip(i - lo, 0, nt - 1), 0)))
        in_specs.append(pl.BlockSpec((tm, d), lambda i, lo=lo, nt=nt, so=off // tm:
                                     (jnp.clip(i - lo, 0, nt - 1) + so, 0)))
        args += [a, res]
        bounds.append((lo, lo + nt))
        lo += nt
    return pl.pallas_call(
        functools.partial(_mm_res_ln_kernel, alpha=alpha, bounds=tuple(bounds)),
        out_shape=jax.ShapeDtypeStruct((lo * tm, d), F32),
        grid=(lo,),
        in_specs=in_specs,
        out_specs=pl.BlockSpec((tm, d), lambda i: (i, 0)),
        compiler_params=_cparams(("arbitrary",)),
        name="mm_res_ln",
    )(*args)


def _mm_qkv_kernel(h_ref, w_ref, b_ref, q_ref, k_ref, v_ref):
    d = k_ref.shape[-1]
    r = jnp.dot(h_ref[...].astype(BF16), w_ref[...], preferred_element_type=F32) + b_ref[...]
    q_ref[...] = r[:, :d].astype(BF16)
    k_ref[...] = r[:, d:2 * d]
    v_ref[...] = r[:, 2 * d:]


def _mm_qkv(h, w_bf, b, n, row_off):
    d = h.shape[1]
    tm = _pick(math.gcd(n, row_off) if row_off else n, (512, 256, 128, 64, 32, 16, 8))
    ro = row_off // tm
    return pl.pallas_call(
        _mm_qkv_kernel,
        out_shape=(jax.ShapeDtypeStruct((n, d), BF16),
                   jax.ShapeDtypeStruct((n, d), F32),
                   jax.ShapeDtypeStruct((n, d), F32)),
        grid=(n // tm,),
        in_specs=[pl.BlockSpec((tm, d), lambda i: (i + ro, 0)),
                  pl.BlockSpec((d, 3 * d), lambda i: (0, 0)),
                  pl.BlockSpec((1, 3 * d), lambda i: (0, 0))],
        out_specs=(pl.BlockSpec((tm, d), lambda i: (i, 0)),) * 3,
        compiler_params=_cparams(("arbitrary",)),
        name="mm_qkv",
    )(h, w_bf, b.reshape(1, -1))


def _conv_kernel(*refs, width, hist, has_state, n_tiles):
    if has_state:
        u_ref, st_ref, w_ref, wb_ref, g_ref, b_ref, y_ref, cs_ref, buf = refs
    else:
        u_ref, w_ref, wb_ref, g_ref, b_ref, y_ref, cs_ref, buf = refs
    t = u_ref.shape[1]
    ti = pl.program_id(1)

    @pl.when(ti == 0)
    def _():
        if has_state:
            buf[0:hist, :] = st_ref[0]
        else:
            buf[0:hist, :] = jnp.zeros((hist, buf.shape[1]), F32)

    buf[hist:hist + t, :] = u_ref[0]
    lo = hist - (width - 1)
    acc = jnp.zeros((t, buf.shape[1]), F32) + wb_ref[...]
    for k in range(width):
        acc = acc + buf[lo + k:lo + k + t, :] * w_ref[k:k + 1, :]
    y = _layer_norm(acc, g_ref[...], b_ref[...])
    y_ref[0] = (y * jax.nn.sigmoid(y)).astype(BF16)
    cs_ref[0] = buf[t + lo:t + hist, :]
    if n_tiles > 1:
        buf[0:hist, :] = buf[t:t + hist, :]


def _conv(u, state, dw_w, dw_b, cn_g, cn_b):
    b, s, d = u.shape
    width = dw_w.shape[0]
    hist = -(-(width - 1) // SUBLANES) * SUBLANES
    t = _pick(s, (512, 256, 128, 64, 32, 16, 8))
    n_tiles = s // t
    assert n_tiles == 1 or t >= hist
    vec = pl.BlockSpec((1, d), lambda i, j: (0, 0))
    in_specs = [pl.BlockSpec((1, t, d), lambda i, j: (i, j, 0))]
    args = [u]
    if state is not None:
        in_specs.append(pl.BlockSpec((1, hist, d), lambda i, j: (i, 0, 0)))
        args.append(state)
    in_specs += [pl.BlockSpec((width, d), lambda i, j: (0, 0)), vec, vec, vec]
    args += [dw_w, dw_b.reshape(1, -1), cn_g.reshape(1, -1), cn_b.reshape(1, -1)]
    return pl.pallas_call(
        functools.partial(_conv_kernel, width=width, hist=hist, has_state=state is not None,
                          n_tiles=n_tiles),
        out_shape=(jax.ShapeDtypeStruct((b, s, d), BF16),
                   jax.ShapeDtypeStruct((b, width - 1, d), F32)),
        grid=(b, n_tiles),
        in_specs=in_specs,
        out_specs=(pl.BlockSpec((1, t, d), lambda i, j: (i, j, 0)),
                   pl.BlockSpec((1, width - 1, d), lambda i, j: (i, 0, 0))),
        scratch_shapes=[pltpu.VMEM((hist + t, d), F32)],
        compiler_params=_cparams(("arbitrary", "arbitrary")),
        name="conv_ln_swish",
    )(*args)


def _sb_block(z, mask, upper, run):
    neg_abs = lax.bitcast_convert_type(lax.bitcast_convert_type(z, jnp.uint32) | jnp.uint32(0x80000000), F32)
    drop = jnp.maximum(z, 0.0) + jnp.log2(1.0 + jnp.exp2(neg_abs))
    log_beta = z - drop
    if mask is not None:
        drop = jnp.where(mask, drop, 0.0)
    later = jnp.dot(drop.astype(BF16), upper, preferred_element_type=F32)
    a = jnp.exp2(log_beta - later - run)
    if mask is not None:
        a = jnp.where(mask, a, 0.0)
    return a.astype(BF16), run + jnp.sum(drop, axis=1, keepdims=True)


def _attn_prompt_kernel(q_ref, k_ref, v_ref, o_ref, kb, vb, *, scale, dh):
    qi = pl.program_id(2)
    blk, width = q_ref.shape
    heads = width // dh

    @pl.when(qi == 0)
    def _():
        kb[...] = k_ref[...].astype(BF16)
        vb[...] = v_ref[...].astype(BF16)

    q = q_ref[...]
    lane = lax.broadcasted_iota(I32, (blk, width), 1)
    row = lax.broadcasted_iota(I32, (blk, blk), 0)
    col = lax.broadcasted_iota(I32, (blk, blk), 1)
    upper = (row > col).astype(BF16)
    causal = col < row
    q_heads = [jnp.where((lane >= h * dh) & (lane < (h + 1) * dh), q, jnp.zeros_like(q))
               for h in range(heads)]

    def block(j, mask, carry):
        off = pl.multiple_of(j * blk, blk)
        ks = kb[pl.ds(off, blk), :]
        vs = vb[pl.ds(off, blk), :]
        out = []
        for h in range(heads):
            acc, run = carry[h]
            z = lax.dot_general(q_heads[h], ks, (((1,), (1,)), ((), ())),
                                preferred_element_type=F32) * scale
            a, run = _sb_block(z, mask, upper, run)
            out.append((acc + jnp.dot(a, vs, preferred_element_type=F32), run))
        return tuple(out)

    init = tuple((jnp.zeros((blk, width), F32), jnp.zeros((blk, 1), F32)) for _ in range(heads))
    carry = block(qi, causal, init)
    carry = lax.fori_loop(0, qi, lambda jj, c: block(qi - 1 - jj, None, c), carry)
    out = carry[0][0]
    for h in range(1, heads):
        out = jnp.where(lane >= h * dh, carry[h][0], out)
    o_ref[...] = out.astype(BF16)


def _attn_prompt(q, k, v, batch, seq, dh):
    n, d = q.shape
    width = LANES
    assert width % dh == 0 and d % width == 0
    blk = _pick(seq, (512, 256, 128))
    nq = seq // blk
    return pl.pallas_call(
        functools.partial(_attn_prompt_kernel, scale=dh ** -0.5 * LOG2_E, dh=dh),
        out_shape=jax.ShapeDtypeStruct((n, d), BF16),
        grid=(batch, d // width, nq),
        in_specs=[pl.BlockSpec((blk, width), lambda b, h, i: (b * nq + i, h)),
                  pl.BlockSpec((seq, width), lambda b, h, i: (b, h)),
                  pl.BlockSpec((seq, width), lambda b, h, i: (b, h))],
        out_specs=pl.BlockSpec((blk, width), lambda b, h, i: (b * nq + i, h)),
        scratch_shapes=[pltpu.VMEM((seq, width), BF16), pltpu.VMEM((seq, width), BF16)],
        compiler_params=_cparams(("arbitrary", "arbitrary", "arbitrary")),
        name="attn_prompt",
    )(q, k, v)


def _attn_sample_kernel(pt_ref, q_ref, kn_ref, vn_ref, *rest, scale, dh, n_heads, ppb):
    del pt_ref
    k_refs, v_refs = rest[:ppb], rest[ppb:2 * ppb]
    o_ref, acc, run, qx = rest[2 * ppb:]
    p = pl.program_id(1)
    t, d = q_ref.shape
    page = k_refs[0].shape[-1]
    m = n_heads * t
    rowh = lax.broadcasted_iota(I32, (m, d), 0) // t
    colh = lax.broadcasted_iota(I32, (m, d), 1) // dh
    own = rowh == colh

    def upper_of(n):
        return (lax.broadcasted_iota(I32, (n, n), 0) > lax.broadcasted_iota(I32, (n, n), 1)).astype(BF16)

    nt = (((1,), (1,)), ((), ()))

    def step(kc, vc, mask, keys_on_lanes):
        if keys_on_lanes:
            z = jnp.dot(qx[...], kc, preferred_element_type=F32) * scale
        else:
            z = lax.dot_general(qx[...], kc, nt, preferred_element_type=F32) * scale
        a, new_run = _sb_block(z, mask, upper_of(z.shape[1]), run[...])
        if keys_on_lanes:
            acc[...] += lax.dot_general(a, vc, nt, preferred_element_type=F32)
        else:
            acc[...] += jnp.dot(a, vc, preferred_element_type=F32)
        run[...] = new_run

    @pl.when(p == 0)
    def _():
        qt = jnp.concatenate([q_ref[...]] * n_heads, axis=0)
        qx[...] = jnp.where(own, qt, jnp.zeros_like(qt))
        acc[...] = jnp.zeros_like(acc)
        run[...] = jnp.zeros_like(run)
        pad = jnp.zeros((page - t, d), BF16)
        kc = jnp.concatenate([kn_ref[...].astype(BF16), pad], axis=0)
        vc = jnp.concatenate([vn_ref[...].astype(BF16), pad], axis=0)
        key = lax.broadcasted_iota(I32, (m, page), 1)
        qpos = lax.broadcasted_iota(I32, (m, page), 0) % t
        step(kc, vc, key < qpos, False)

    def columns(ref):
        return ref[0, 0].reshape(d, page).astype(BF16)

    kc = jnp.concatenate([columns(k_refs[r]) for r in reversed(range(ppb))], axis=1)
    vc = jnp.concatenate([columns(v_refs[r]) for r in reversed(range(ppb))], axis=1)
    step(kc, vc, None, True)

    @pl.when(p == pl.num_programs(1) - 1)
    def _():
        sel = jnp.where(own, acc[...], 0.0).reshape(n_heads, t, d)
        o_ref[...] = jnp.sum(sel, axis=0).astype(BF16)


def _attn_sample(q, k_new, v_new, cache_k, cache_v, layer, page_table):
    n, d = q.shape
    db, n_pages = page_table.shape
    t = n // db
    _, _, page, n_heads, dh = cache_k.shape
    ppb = _pick(n_pages, (8, 4, 2, 1))
    steps = n_pages // ppb
    assert t <= page and n_heads * dh == d
    cache_k = jnp.transpose(cache_k, (0, 1, 3, 4, 2))
    cache_v = jnp.transpose(cache_v, (0, 1, 3, 4, 2))

    def page_spec(r):
        return pl.BlockSpec((1, 1, n_heads, dh, page),
                            lambda b, p, pt: (layer, pt[b, n_pages - 1 - (p * ppb + r)], 0, 0, 0))

    tok = pl.BlockSpec((t, d), lambda b, p, pt: (b, 0))
    grid_spec = pltpu.PrefetchScalarGridSpec(
        num_scalar_prefetch=1,
        grid=(db, steps),
        in_specs=[tok, tok, tok] + [page_spec(r) for r in range(ppb)] * 2,
        out_specs=tok,
        scratch_shapes=[pltpu.VMEM((n_heads * t, d), F32), pltpu.VMEM((n_heads * t, 1), F32),
                        pltpu.VMEM((n_heads * t, d), BF16)],
    )
    return pl.pallas_call(
        functools.partial(_attn_sample_kernel, scale=dh ** -0.5 * LOG2_E, dh=dh, n_heads=n_heads, ppb=ppb),
        out_shape=jax.ShapeDtypeStruct((n, d), BF16),
        grid_spec=grid_spec,
        compiler_params=_cparams(("arbitrary", "arbitrary")),
        name="attn_sample",
    )(page_table, q, k_new, v_new, *([cache_k] * ppb), *([cache_v] * ppb))


def _router_kernel(h_ref, w_ref, b_ref, idx_ref, gate_ref, rank_ref, cnt_ref, carry):
    i = pl.program_id(0)

    @pl.when(i == 0)
    def _():
        carry[...] = jnp.zeros_like(carry)

    logits = jnp.dot(h_ref[...], w_ref[...], preferred_element_type=F32,
                     precision=lax.Precision.HIGHEST) + b_ref[...]
    t, ep = logits.shape
    lane = lax.broadcasted_iota(I32, (t, ep), 1)
    left = logits
    chosen = jnp.zeros((t, ep), F32)
    vals, idxs = [], []
    for _ in range(TOP_K):
        top = jnp.max(left, axis=-1, keepdims=True)
        idx = jnp.min(jnp.where(left == top, lane, ep), axis=-1, keepdims=True)
        sel = lane == idx
        vals.append(top)
        idxs.append(idx)
        chosen = jnp.where(sel, 1.0, chosen)
        left = jnp.where(sel, -jnp.inf, left)
    exps = [jnp.exp(v - vals[0]) for v in vals]
    denom = exps[0]
    for e in exps[1:]:
        denom = denom + e
    lower = (lax.broadcasted_iota(I32, (t, t), 0) > lax.broadcasted_iota(I32, (t, t), 1)).astype(BF16)
    before = jnp.dot(lower, chosen.astype(BF16), preferred_element_type=F32) + carry[...]
    idx_o = jnp.zeros((t, ep), I32)
    gate_o = jnp.zeros((t, ep), F32)
    rank_o = jnp.zeros((t, ep), I32)
    for k in range(TOP_K):
        rank_k = jnp.sum(jnp.where(lane == idxs[k], before, 0.0), axis=-1, keepdims=True)
        idx_o = jnp.where(lane == k, idxs[k], idx_o)
        gate_o = jnp.where(lane == k, exps[k] / denom, gate_o)
        rank_o = jnp.where(lane == k, rank_k.astype(I32), rank_o)
    idx_ref[...] = idx_o[:, :TOP_K]
    gate_ref[...] = gate_o[:, :TOP_K]
    rank_ref[...] = rank_o[:, :TOP_K]
    carry[...] += jnp.sum(chosen, axis=0, keepdims=True)
    cnt_ref[...] = carry[...].astype(I32)


def _router(h, w_pad, b_pad):
    n, d = h.shape
    ep = w_pad.shape[1]
    t = _pick(n, (256, 128, 64, 32))
    small = pl.BlockSpec((t, TOP_K), lambda i: (i, 0))
    return pl.pallas_call(
        _router_kernel,
        out_shape=(jax.ShapeDtypeStruct((n, TOP_K), I32), jax.ShapeDtypeStruct((n, TOP_K), F32),
                   jax.ShapeDtypeStruct((n, TOP_K), I32), jax.ShapeDtypeStruct((1, ep), I32)),
        grid=(n // t,),
        in_specs=[pl.BlockSpec((t, d), lambda i: (i, 0)),
                  pl.BlockSpec((d, ep), lambda i: (0, 0)),
                  pl.BlockSpec((1, ep), lambda i: (0, 0))],
        out_specs=(small, small, small, pl.BlockSpec((1, ep), lambda i: (0, 0))),
        scratch_shapes=[pltpu.VMEM((1, ep), F32)],
        compiler_params=_cparams(("arbitrary",)),
        name="moe_router",
    )(h, w_pad, b_pad)


def _dispatch_kernel(pstart_ref, pad_ref, h_ref, idx_hbm, rank_hbm, xs_hbm,
                     idx_s0, idx_s1, rank_s0, rank_s1, rows0, rows1, zblk, sem_m, sem_r, sem_z, *, n_exp):
    i = pl.program_id(0)
    n_steps = pl.num_programs(0)
    td, d = h_ref.shape
    na = TOP_K * td
    bm = zblk.shape[0]
    idx_s, rank_s, rows = (idx_s0, idx_s1), (rank_s0, rank_s1), (rows0, rows1)

    def meta(step, slot):
        return (pltpu.make_async_copy(idx_hbm.at[pl.ds(step * td, td)], idx_s[slot], sem_m.at[0, slot]),
                pltpu.make_async_copy(rank_hbm.at[pl.ds(step * td, td)], rank_s[slot], sem_m.at[1, slot]))

    def wait_rows(slot):
        pltpu.make_async_copy(xs_hbm.at[pl.ds(0, na)], xs_hbm.at[pl.ds(0, na)], sem_r.at[slot]).wait()

    @pl.when(i == 0)
    def _():
        zblk[...] = jnp.zeros_like(zblk)
        total = pstart_ref[n_exp - 1] + pad_ref[n_exp - 1]
        n_tail = (xs_hbm.shape[0] - total) // bm

        def zero_copy(start):
            return pltpu.make_async_copy(zblk, xs_hbm.at[pl.ds(start, bm)], sem_z)

        def fill(e, _):
            @pl.when(pad_ref[e] > 0)
            def _():
                zero_copy(pstart_ref[e] + pad_ref[e] - bm).start()
            return 0

        def drain(e, _):
            @pl.when(pad_ref[e] > 0)
            def _():
                zero_copy(0).wait()
            return 0

        lax.fori_loop(0, n_exp, fill, 0)
        lax.fori_loop(0, n_tail, lambda j, c: (zero_copy(total + j * bm).start(), c)[1], 0)
        lax.fori_loop(0, n_exp, drain, 0)
        lax.fori_loop(0, n_tail, lambda j, c: (zero_copy(0).wait(), c)[1], 0)
        for c in meta(0, 0):
            c.start()

    def step(slot):
        @pl.when(i + 1 < n_steps)
        def _():
            for c in meta(i + 1, 1 - slot):
                c.start()

        for c in meta(i, slot):
            c.wait()
        rows[slot][...] = h_ref[...].reshape(td, 1, d)

        def row(t, _):
            for k in range(TOP_K):
                dst = pstart_ref[idx_s[slot][t, k]] + rank_s[slot][t, k]
                pltpu.make_async_copy(rows[slot].at[t], xs_hbm.at[dst], sem_r.at[slot]).start(priority=k % 2)
            return 0

        lax.fori_loop(0, td, row, 0)

        @pl.when(i > 0)
        def _():
            wait_rows(1 - slot)

        @pl.when(i == n_steps - 1)
        def _():
            wait_rows(slot)

    for slot in range(2):
        pl.when(i % 2 == slot)(functools.partial(step, slot))


def _dispatch(h, idx, rank, pstart, padded, cap, bm):
    n, d = h.shape
    n_exp = pstart.shape[0]
    td = _pick(n, (256, 128, 64, 32))
    grid_spec = pltpu.PrefetchScalarGridSpec(
        num_scalar_prefetch=2,
        grid=(n // td,),
        in_specs=[pl.BlockSpec((td, d), lambda i, ps, ct: (i, 0)),
                  pl.BlockSpec(memory_space=pl.ANY),
                  pl.BlockSpec(memory_space=pl.ANY)],
        out_specs=pl.BlockSpec(memory_space=pl.ANY),
        scratch_shapes=[pltpu.SMEM((td, TOP_K), I32)] * 4
        + [pltpu.VMEM((td, 1, d), F32), pltpu.VMEM((td, 1, d), F32), pltpu.VMEM((bm, 1, d), F32),
           pltpu.SemaphoreType.DMA((2, 2)), pltpu.SemaphoreType.DMA((2,)), pltpu.SemaphoreType.DMA],
    )
    return pl.pallas_call(
        functools.partial(_dispatch_kernel, n_exp=n_exp),
        out_shape=jax.ShapeDtypeStruct((cap, 1, d), F32),
        grid_spec=grid_spec,
        compiler_params=_cparams(("arbitrary",)),
        name="moe_dispatch",
    )(pstart, padded, h, idx, rank)


def _experts_kernel(be_ref, nvb_ref, xs_ref, w1_ref, b1_ref, w2_ref, b2_ref, ys_ref, w1b, w2b, x2d):
    i = pl.program_id(0)
    valid = i < nvb_ref[0]
    f, d = w2b.shape
    bm = x2d.shape[0]
    fresh = jnp.logical_or(i == 0, be_ref[i] != be_ref[jnp.maximum(i - 1, 0)])

    @pl.when(jnp.logical_and(valid, fresh))
    def _():
        w1b[...] = w1_ref[0, 0].astype(BF16)
        w2b[...] = w2_ref[0, 0].astype(BF16)

    @pl.when(valid)
    def _():
        x2d[...] = xs_ref[...].reshape(bm, d)
        h = jnp.dot(x2d[...].astype(BF16), w1b[...], preferred_element_type=F32) + b1_ref[0, 0]
        gate = jnp.minimum(h[:, :f], SWIGLU_LIMIT)
        up = jnp.clip(h[:, f:], -SWIGLU_LIMIT, SWIGLU_LIMIT)
        act = (up + 1.0) * gate * jax.nn.sigmoid(SWIGLU_ALPHA * gate)
        y = jnp.dot(act.astype(BF16), w2b[...], preferred_element_type=F32) + b2_ref[0, 0]
        ys_ref[...] = y.reshape(bm, 1, d)

    @pl.when(jnp.logical_not(valid))
    def _():
        ys_ref[...] = jnp.zeros_like(ys_ref)


def _experts(xs, block_e, n_valid, w1, b1, w2, b2, layer, n_blocks, bm):
    d = xs.shape[-1]
    _, e, _, f2 = w1.shape
    f = f2 // 2
    b1 = b1.reshape(-1, e, 1, f2)
    b2 = b2.reshape(-1, e, 1, d)
    grid_spec = pltpu.PrefetchScalarGridSpec(
        num_scalar_prefetch=2,
        grid=(n_blocks,),
        in_specs=[pl.BlockSpec((bm, 1, d), lambda i, be, nv: (jnp.minimum(i, nv[0] - 1), 0, 0)),
                  pl.BlockSpec((1, 1, d, f2), lambda i, be, nv: (layer, be[i], 0, 0)),
                  pl.BlockSpec((1, 1, 1, f2), lambda i, be, nv: (layer, be[i], 0, 0)),
                  pl.BlockSpec((1, 1, f, d), lambda i, be, nv: (layer, be[i], 0, 0)),
                  pl.BlockSpec((1, 1, 1, d), lambda i, be, nv: (layer, be[i], 0, 0))],
        out_specs=pl.BlockSpec((bm, 1, d), lambda i, be, nv: (i, 0, 0)),
        scratch_shapes=[pltpu.VMEM((d, f2), BF16), pltpu.VMEM((f, d), BF16), pltpu.VMEM((bm, d), F32)],
    )
    return pl.pallas_call(
        _experts_kernel,
        out_shape=jax.ShapeDtypeStruct((n_blocks * bm, 1, d), F32),
        grid_spec=grid_spec,
        compiler_params=_cparams(("arbitrary",)),
        name="moe_experts",
    )(block_e, n_valid, xs, w1, b1, w2, b2)


def _combine_kernel(pstart_ref, h_ref, gate_ref, idx_hbm, rank_hbm, ys_hbm, g_ref, b_ref, o_ref,
                    idx_s0, idx_s1, rank_s0, rank_s1, buf0, buf1, y2d, sem_m, sem_r, *, alpha, blk_off):
    i = pl.program_id(0)
    n_steps = pl.num_programs(0)
    tf, d = h_ref.shape
    idx_s, rank_s, buf = (idx_s0, idx_s1), (rank_s0, rank_s1), (buf0, buf1)

    def meta(step, slot):
        rows = pl.ds((step + blk_off) * tf, tf)
        return (pltpu.make_async_copy(idx_hbm.at[rows], idx_s[slot], sem_m.at[0, slot]),
                pltpu.make_async_copy(rank_hbm.at[rows], rank_s[slot], sem_m.at[1, slot]))

    def gather(slot):
        def row(t, _):
            for k in range(TOP_K):
                src = pstart_ref[idx_s[slot][t, k]] + rank_s[slot][t, k]
                pltpu.make_async_copy(ys_hbm.at[src], buf[slot].at[k * tf + t], sem_r.at[slot]).start()
            return 0

        lax.fori_loop(0, tf, row, 0)

    @pl.when(i == 0)
    def _():
        for c in meta(0, 0):
            c.start()
        for c in meta(0, 0):
            c.wait()
        gather(0)

        @pl.when(n_steps > 1)
        def _():
            for c in meta(1, 1):
                c.start()

    def step(slot):
        @pl.when(i + 1 < n_steps)
        def _():
            for c in meta(i + 1, 1 - slot):
                c.wait()
            gather(1 - slot)

        @pl.when(i + 2 < n_steps)
        def _():
            for c in meta(i + 2, slot):
                c.start()

        pltpu.make_async_copy(buf[slot], buf[slot], sem_r.at[slot]).wait()
        y = alpha * h_ref[...]
        gates = gate_ref[...]
        for k in range(TOP_K):
            y2d[...] = buf[slot][pl.ds(k * tf, tf)].reshape(tf, d)
            y = y + gates[:, k:k + 1] * y2d[...]
        o_ref[...] = _layer_norm(y, g_ref[...], b_ref[...])

    for slot in range(2):
        pl.when(i % 2 == slot)(functools.partial(step, slot))


def _combine(h, gates, idx, rank, ys, pstart, g, beta, alpha, n, row_off):
    d = h.shape[1]
    tf = _pick(math.gcd(n, row_off) if row_off else n, (256, 128, 64, 32))
    bo = row_off // tf
    grid_spec = pltpu.PrefetchScalarGridSpec(
        num_scalar_prefetch=1,
        grid=(n // tf,),
        in_specs=[pl.BlockSpec((tf, d), lambda i, ps: (i + bo, 0)),
                  pl.BlockSpec((tf, TOP_K), lambda i, ps: (i + bo, 0)),
                  pl.BlockSpec(memory_space=pl.ANY),
                  pl.BlockSpec(memory_space=pl.ANY),
                  pl.BlockSpec(memory_space=pl.ANY),
                  pl.BlockSpec((1, d), lambda i, ps: (0, 0)),
                  pl.BlockSpec((1, d), lambda i, ps: (0, 0))],
        out_specs=pl.BlockSpec((tf, d), lambda i, ps: (i, 0)),
        scratch_shapes=[pltpu.SMEM((tf, TOP_K), I32)] * 4
        + [pltpu.VMEM((TOP_K * tf, 1, d), F32), pltpu.VMEM((TOP_K * tf, 1, d), F32), pltpu.VMEM((tf, d), F32),
           pltpu.SemaphoreType.DMA((2, 2)), pltpu.SemaphoreType.DMA((2,))],
    )
    return pl.pallas_call(
        functools.partial(_combine_kernel, alpha=alpha, blk_off=bo),
        out_shape=jax.ShapeDtypeStruct((n, d), F32),
        grid_spec=grid_spec,
        compiler_params=_cparams(("arbitrary",)),
        name="moe_combine_ln",
    )(pstart, h, gates, idx, rank, ys, g.reshape(1, -1), beta.reshape(1, -1))


def _moe_ln(h, w_router, b_router, w1, b1, w2, b2, layer, g, beta, alpha, splits):
    n, d = h.shape
    n_exp = w_router.shape[1]
    ep = -(-n_exp // LANES) * LANES
    bm = EXPERT_BLOCK
    w_pad = jnp.zeros((d, ep), F32).at[:, :n_exp].set(w_router)
    b_pad = jnp.full((1, ep), NEG_BIG, F32).at[0, :n_exp].set(b_router)
    idx, gates, rank, counts = _router(h, w_pad, b_pad)
    counts = counts[0, :n_exp]
    padded = (counts + bm - 1) // bm * bm
    ends = jnp.cumsum(padded)
    pstart = (ends - padded).astype(I32)
    n_blocks = -(-(n * TOP_K + n_exp * (bm - 1)) // bm)
    starts = jnp.arange(n_blocks, dtype=I32) * bm
    block_e = jnp.minimum(jnp.sum(starts[:, None] >= ends[None, :], axis=1), n_exp - 1).astype(I32)
    n_valid = (ends[-1:] // bm).astype(I32)
    xs = _dispatch(h, idx, rank, pstart, padded.astype(I32), n_blocks * bm, bm)
    ys = _experts(xs, block_e, n_valid, w1, b1, w2, b2, layer, n_blocks, bm)
    return [_combine(h, gates, idx, rank, ys, pstart, g, beta, alpha, rows, off)
            for off, rows in splits]


def kernel(x_prompt, x_sample, state_conv, cache_k, cache_v, page_table, w_pw1, b_pw1, dw_w, dw_b, cn_g, cn_b, w_pw2, b_pw2, w_qkv, b_qkv, w_o, ln_mix_g, ln_mix_b, ln_ffn_g, ln_ffn_b, w_router, b_router, w_e1, b_e1, w_e2, b_e2):
    b, s, d = x_prompt.shape
    db, t, _ = x_sample.shape
    depth = ln_mix_g.shape[0]
    n_heads, dh = cache_k.shape[3], cache_k.shape[4]
    width = dw_w.shape[1]
    n_p, n_s = b * s, db * t
    n = n_p + n_s
    alpha = (2 * depth) ** 0.25
    hist = -(-(width - 1) // SUBLANES) * SUBLANES

    xp, xs_ = x_prompt.reshape(n_p, d), x_sample.reshape(n_s, d)
    h = None
    conv_p, conv_s, k_p, v_p, k_s, v_s = [], [], [], [], [], []
    for i in range(depth):
        j = i // 2
        last = i == depth - 1
        if i % 2 == 0:
            w1_bf = w_pw1[j].astype(BF16)
            up = _mm_glu(xp if h is None else h[:n_p], w1_bf, b_pw1[j])
            us = _mm_glu(xs_ if h is None else h[n_p:], w1_bf, b_pw1[j])
            state = jnp.pad(state_conv[j].astype(F32), ((0, 0), (hist - (width - 1), 0), (0, 0)))
            yp, cp = _conv(up.reshape(b, s, d), None, dw_w[j], dw_b[j], cn_g[j], cn_b[j])
            ys, cs = _conv(us.reshape(db, t, d), state, dw_w[j], dw_b[j], cn_g[j], cn_b[j])
            conv_p.append(cp)
            conv_s.append(cs)
            w2_bf = w_pw2[j].astype(BF16)
            res = [(xp, 0), (xs_, 0)] if h is None else [(h, 0), (h, n_p)]
            segs = [(yp.reshape(n_p, d), *res[0]), (ys.reshape(n_s, d), *res[1])]
            hm = _mm_res_ln(segs, w2_bf, b_pw2[j], ln_mix_g[i], ln_mix_b[i], alpha)
        else:
            wq_bf = w_qkv[j].astype(BF16)
            qp, kp, vp = _mm_qkv(h, wq_bf, b_qkv[j], n_p, 0)
            qs, ks, vs = _mm_qkv(h, wq_bf, b_qkv[j], n_s, n_p)
            op = _attn_prompt(qp, kp, vp, b, s, dh)
            os_ = _attn_sample(qs, ks, vs, cache_k, cache_v, j, page_table)
            k_p.append(kp.reshape(b, s, n_heads, dh))
            v_p.append(vp.reshape(b, s, n_heads, dh))
            k_s.append(ks.reshape(db, t, n_heads, dh))
            v_s.append(vs.reshape(db, t, n_heads, dh))
            hm = _mm_res_ln([(op, h, 0), (os_, h, n_p)], w_o[j].astype(BF16), jnp.zeros((d,), F32),
                            ln_mix_g[i], ln_mix_b[i], alpha)
        splits = [(0, n_p), (n_p, n_s)] if last else [(0, n)]
        outs = _moe_ln(hm, w_router[i], b_router[i], w_e1, b_e1, w_e2, b_e2, i,
                       ln_ffn_g[i], ln_ffn_b[i], alpha, splits)
        if last:
            y_p, y_s = outs
        else:
            h = outs[0]
    return (y_p.reshape(b, s, d), y_s.reshape(db, t, d), jnp.stack(conv_p), jnp.stack(conv_s),
            jnp.stack(k_p), jnp.stack(v_p), jnp.stack(k_s), jnp.stack(v_s))
```

```python
import functools
import math

import jax
import jax.numpy as jnp
from jax import lax
from jax.experimental import pallas as pl
from jax.experimental.pallas import tpu as pltpu

F32 = jnp.float32
BF16 = jnp.bfloat16
I32 = jnp.int32

TOP_K = 4
SWIGLU_LIMIT = 7.0
SWIGLU_ALPHA = 1.702
LN_EPS = 1e-5
LANES = 128
SUBLANES = 8
VMEM_LIMIT = 56 * 1024 * 1024
EXPERT_BLOCK = 256
NEG_BIG = -1e30
LOG2_E = math.log2(math.e)


def _pick(n, cands):
    for c in cands:
        if n % c == 0:
            return c
    raise ValueError(f"no tile in {cands} divides {n}")


def _cparams(sem):
    return pltpu.CompilerParams(dimension_semantics=sem, vmem_limit_bytes=VMEM_LIMIT)


def _layer_norm(x, g, b):
    mu = jnp.mean(x, axis=-1, keepdims=True)
    xc = x - mu
    var = jnp.mean(xc * xc, axis=-1, keepdims=True)
    return xc * lax.rsqrt(var + LN_EPS) * g + b


def _mm_glu_kernel(x_ref, w_ref, b_ref, u_ref):
    d = u_ref.shape[-1]
    h = jnp.dot(x_ref[...].astype(BF16), w_ref[...], preferred_element_type=F32) + b_ref[...]
    u_ref[...] = h[:, :d] * jax.nn.sigmoid(h[:, d:])


def _mm_glu(x, w_bf, b):
    n, d = x.shape
    tm = _pick(n, (512, 256, 128, 64, 32, 16, 8))
    return pl.pallas_call(
        _mm_glu_kernel,
        out_shape=jax.ShapeDtypeStruct((n, d), F32),
        grid=(n // tm,),
        in_specs=[pl.BlockSpec((tm, d), lambda i: (i, 0)),
                  pl.BlockSpec((d, 2 * d), lambda i: (0, 0)),
                  pl.BlockSpec((1, 2 * d), lambda i: (0, 0))],
        out_specs=pl.BlockSpec((tm, d), lambda i: (i, 0)),
        compiler_params=_cparams(("arbitrary",)),
        name="mm_glu",
    )(x, w_bf, b.reshape(1, -1))


def _mm_res_ln_kernel(w_ref, b_ref, g_ref, beta_ref, *rest, alpha, bounds):
    o_ref = rest[-1]
    i = pl.program_id(0)
    for s, (lo, hi) in enumerate(bounds):
        a_ref, res_ref = rest[2 * s], rest[2 * s + 1]

        @pl.when(jnp.logical_and(i >= lo, i < hi))
        def _(a_ref=a_ref, res_ref=res_ref):
            m = jnp.dot(a_ref[...].astype(BF16), w_ref[...], preferred_element_type=F32) + b_ref[...]
            o_ref[...] = _layer_norm(alpha * res_ref[...] + m, g_ref[...], beta_ref[...])


def _mm_res_ln(segs, w_bf, b, g, beta, alpha):
    k, d = w_bf.shape
    tm = functools.reduce(math.gcd, [x for a, _, off in segs for x in (a.shape[0], off)])
    tm = _pick(tm, (512, 256, 128, 64, 32, 16, 8))
    vec = pl.BlockSpec((1, d), lambda i: (0, 0))
    in_specs = [pl.BlockSpec((k, d), lambda i: (0, 0)), vec, vec, vec]
    args = [w_bf, b.reshape(1, -1), g.reshape(1, -1), beta.reshape(1, -1)]
    bounds, lo = [], 0
    for a, res, off in segs:
        nt = a.shape[0] // tm
        in_specs.append(pl.BlockSpec((tm, k), lambda i, lo=lo, nt=nt: (jnp.clip(i - lo, 0, nt - 1), 0)))
        in_specs.append(pl.BlockSpec((tm, d), lambda i, lo=lo, nt=nt, so=off // tm:
                                     (jnp.clip(i - lo, 0, nt - 1) + so, 0)))
        args += [a, res]
        bounds.append((lo, lo + nt))
        lo += nt
    return pl.pallas_call(
        functools.partial(_mm_res_ln_kernel, alpha=alpha, bounds=tuple(bounds)),
        out_shape=jax.ShapeDtypeStruct((lo * tm, d), F32),
        grid=(lo,),
        in_specs=in_specs,
        out_specs=pl.BlockSpec((tm, d), lambda i: (i, 0)),
        compiler_params=_cparams(("arbitrary",)),
        name="mm_res_ln",
    )(*args)


def _mm_qkv_kernel(h_ref, w_ref, b_ref, q_ref, k_ref, v_ref):
    d = k_ref.shape[-1]
    r = jnp.dot(h_ref[...].astype(BF16), w_ref[...], preferred_element_type=F32) + b_ref[...]
    q_ref[...] = r[:, :d].astype(BF16)
    k_ref[...] = r[:, d:2 * d]
    v_ref[...] = r[:, 2 * d:]


def _mm_qkv_t_kernel(h_ref, wq_ref, bq_ref, wkv_ref, bkv_ref, q_ref, kt_ref, vt_ref):
    d = q_ref.shape[-1]
    hb = h_ref[...].astype(BF16)
    q_ref[...] = (jnp.dot(hb, wq_ref[...], preferred_element_type=F32) + bq_ref[...]).astype(BF16)
    kv = lax.dot_general(wkv_ref[...], hb, (((1,), (1,)), ((), ())), preferred_element_type=F32) + bkv_ref[...]
    kt_ref[0] = kv[:d]
    vt_ref[0] = kv[d:]


def _mm_qkv_t(h, w_bf, b, batch, seq):
    d = h.shape[1]
    tm = _pick(seq, (512, 256, 128))
    nt = seq // tm
    wq, wkv_t = w_bf[:, :d], w_bf[:, d:].T
    bq, bkv = b[:d].reshape(1, d), b[d:].reshape(2 * d, 1)
    const = lambda shape: pl.BlockSpec(shape, lambda bi, i: (0, 0))
    return pl.pallas_call(
        _mm_qkv_t_kernel,
        out_shape=(jax.ShapeDtypeStruct((batch * seq, d), BF16),
                   jax.ShapeDtypeStruct((batch, d, seq), F32),
                   jax.ShapeDtypeStruct((batch, d, seq), F32)),
        grid=(batch, nt),
        in_specs=[pl.BlockSpec((tm, d), lambda bi, i: (bi * nt + i, 0)),
                  const((d, d)), const((1, d)), const((2 * d, d)), const((2 * d, 1))],
        out_specs=(pl.BlockSpec((tm, d), lambda bi, i: (bi * nt + i, 0)),
                   pl.BlockSpec((1, d, tm), lambda bi, i: (bi, 0, i)),
                   pl.BlockSpec((1, d, tm), lambda bi, i: (bi, 0, i))),
        compiler_params=_cparams(("arbitrary", "arbitrary")),
        name="mm_qkv_t",
    )(h, wq, bq, wkv_t, bkv)


def _mm_qkv(h, w_bf, b, n, row_off):
    d = h.shape[1]
    tm = _pick(math.gcd(n, row_off) if row_off else n, (512, 256, 128, 64, 32, 16, 8))
    ro = row_off // tm
    return pl.pallas_call(
        _mm_qkv_kernel,
        out_shape=(jax.ShapeDtypeStruct((n, d), BF16),
                   jax.ShapeDtypeStruct((n, d), F32),
                   jax.ShapeDtypeStruct((n, d), F32)),
        grid=(n // tm,),
        in_specs=[pl.BlockSpec((tm, d), lambda i: (i + ro, 0)),
                  pl.BlockSpec((d, 3 * d), lambda i: (0, 0)),
                  pl.BlockSpec((1, 3 * d), lambda i: (0, 0))],
        out_specs=(pl.BlockSpec((tm, d), lambda i: (i, 0)),) * 3,
        compiler_params=_cparams(("arbitrary",)),
        name="mm_qkv",
    )(h, w_bf, b.reshape(1, -1))


def _conv_kernel(*refs, width, hist, has_state, n_tiles):
    if has_state:
        u_ref, st_ref, w_ref, wb_ref, g_ref, b_ref, y_ref, cs_ref, buf, conv, taps = refs
    else:
        u_ref, w_ref, wb_ref, g_ref, b_ref, y_ref, cs_ref, buf, conv, taps = refs
    t = u_ref.shape[1]
    d = u_ref.shape[2]
    ti = pl.program_id(1)
    n_shift = buf.shape[0]
    span = hist + t - n_shift

    @pl.when(ti == 0)
    def _():
        if has_state:
            buf[0, 0:hist, :] = st_ref[0]
        else:
            buf[0, 0:hist, :] = jnp.zeros((hist, d), F32)

    buf[0, hist:hist + t, :] = u_ref[0]
    for r in range(1, n_shift):
        buf[r, 0:span, :] = buf[0, r:r + span, :]
    lo = hist - (width - 1)
    rows = min(t, 2 * SUBLANES)

    @pl.when(jnp.logical_and(pl.program_id(0) == 0, ti == 0))
    def _():
        for k in range(width):
            taps[k] = jnp.zeros((rows, d), F32) + w_ref[k:k + 1, :]

    def chunk(c, _):
        base = pl.multiple_of(c * rows, rows)
        acc = jnp.zeros((rows, d), F32) + wb_ref[...]
        for k in range(width):
            o = lo + k
            win = buf[o % n_shift, pl.ds(base + (o // n_shift) * n_shift, rows), :]
            acc = acc + win * taps[k]
        conv[pl.ds(base, rows), :] = acc
        return 0

    lax.fori_loop(0, t // rows, chunk, 0)
    y = _layer_norm(conv[...], g_ref[...], b_ref[...])
    y_ref[0] = (y * jax.nn.sigmoid(y)).astype(BF16)
    cs_ref[0] = buf[0, t + lo:t + hist, :]
    if n_tiles > 1:
        buf[0, 0:hist, :] = buf[0, t:t + hist, :]


def _conv(u, state, dw_w, dw_b, cn_g, cn_b):
    b, s, d = u.shape
    width = dw_w.shape[0]
    hist = -(-(width - 1) // SUBLANES) * SUBLANES
    t = _pick(s, (512, 256, 128, 64, 32, 16, 8))
    n_tiles = s // t
    assert n_tiles == 1 or t >= hist
    vec = pl.BlockSpec((1, d), lambda i, j: (0, 0))
    in_specs = [pl.BlockSpec((1, t, d), lambda i, j: (i, j, 0))]
    args = [u]
    if state is not None:
        in_specs.append(pl.BlockSpec((1, hist, d), lambda i, j: (i, 0, 0)))
        args.append(state)
    in_specs += [pl.BlockSpec((width, d), lambda i, j: (0, 0)), vec, vec, vec]
    args += [dw_w, dw_b.reshape(1, -1), cn_g.reshape(1, -1), cn_b.reshape(1, -1)]
    return pl.pallas_call(
        functools.partial(_conv_kernel, width=width, hist=hist, has_state=state is not None,
                          n_tiles=n_tiles),
        out_shape=(jax.ShapeDtypeStruct((b, s, d), BF16),
                   jax.ShapeDtypeStruct((b, width - 1, d), F32)),
        grid=(b, n_tiles),
        in_specs=in_specs,
        out_specs=(pl.BlockSpec((1, t, d), lambda i, j: (i, j, 0)),
                   pl.BlockSpec((1, width - 1, d), lambda i, j: (i, 0, 0))),
        scratch_shapes=[pltpu.VMEM((SUBLANES, hist + t, d), F32), pltpu.VMEM((t, d), F32),
                        pltpu.VMEM((width, min(t, 2 * SUBLANES), d), F32)],
        compiler_params=_cparams(("arbitrary", "arbitrary")),
        name="conv_ln_swish",
    )(*args)


def _sb_block(z, mask, upper, run):
    neg_abs = lax.bitcast_convert_type(lax.bitcast_convert_type(z, jnp.uint32) | jnp.uint32(0x80000000), F32)
    drop = jnp.maximum(z, 0.0) + jnp.log2(1.0 + jnp.exp2(neg_abs))
    log_beta = z - drop
    if mask is not None:
        drop = jnp.where(mask, drop, 0.0)
    later = jnp.dot(drop.astype(BF16), upper, preferred_element_type=F32)
    a = jnp.exp2(log_beta - later - run)
    if mask is not None:
        a = jnp.where(mask, a, 0.0)
    return a.astype(BF16), run + jnp.sum(drop, axis=1, keepdims=True)


def _attn_prompt_kernel(q_ref, k_ref, v_ref, o_ref, kb, vb, *, scale, dh):
    qi = pl.program_id(2)
    blk, width = q_ref.shape
    heads = width // dh

    @pl.when(qi == 0)
    def _():
        for jj in range(kb.shape[0]):
            kb[jj] = k_ref[0, :, jj * blk:(jj + 1) * blk].astype(BF16)
            vb[jj] = v_ref[0, :, jj * blk:(jj + 1) * blk].astype(BF16)

    q = q_ref[...]
    lane = lax.broadcasted_iota(I32, (blk, width), 1)
    row = lax.broadcasted_iota(I32, (blk, blk), 0)
    col = lax.broadcasted_iota(I32, (blk, blk), 1)
    upper = (row > col).astype(BF16)
    causal = col < row
    q_heads = [jnp.where((lane >= h * dh) & (lane < (h + 1) * dh), q, jnp.zeros_like(q))
               for h in range(heads)]

    def block(j, mask, carry):
        ks = kb[j]
        vs = vb[j]
        out = []
        for h in range(heads):
            acc, run = carry[h]
            z = jnp.dot(q_heads[h], ks, preferred_element_type=F32) * scale
            a, run = _sb_block(z, mask, upper, run)
            pv = lax.dot_general(a, vs, (((1,), (1,)), ((), ())), preferred_element_type=F32)
            out.append((acc + pv, run))
        return tuple(out)

    init = tuple((jnp.zeros((blk, width), F32), jnp.zeros((blk, 1), F32)) for _ in range(heads))
    carry = block(qi, causal, init)
    carry = lax.fori_loop(0, qi, lambda jj, c: block(qi - 1 - jj, None, c), carry)
    out = carry[0][0]
    for h in range(1, heads):
        out = jnp.where(lane >= h * dh, carry[h][0], out)
    o_ref[...] = out.astype(BF16)


def _attn_prompt(q, k_t, v_t, dh):
    n, d = q.shape
    batch, _, seq = k_t.shape
    width = LANES
    assert width % dh == 0 and d % width == 0
    blk = _pick(seq, (512, 256, 128))
    nq = seq // blk
    return pl.pallas_call(
        functools.partial(_attn_prompt_kernel, scale=dh ** -0.5 * LOG2_E, dh=dh),
        out_shape=jax.ShapeDtypeStruct((n, d), BF16),
        grid=(batch, d // width, nq),
        in_specs=[pl.BlockSpec((blk, width), lambda b, h, i: (b * nq + i, h)),
                  pl.BlockSpec((1, width, seq), lambda b, h, i: (b, h, 0)),
                  pl.BlockSpec((1, width, seq), lambda b, h, i: (b, h, 0))],
        out_specs=pl.BlockSpec((blk, width), lambda b, h, i: (b * nq + i, h)),
        scratch_shapes=[pltpu.VMEM((nq, width, blk), BF16), pltpu.VMEM((nq, width, blk), BF16)],
        compiler_params=_cparams(("arbitrary", "arbitrary", "arbitrary")),
        name="attn_prompt",
    )(q, k_t, v_t)


def _attn_sample_kernel(pt_ref, q_ref, kn_ref, vn_ref, *rest, scale, dh, n_heads, ppb):
    del pt_ref
    k_refs, v_refs = rest[:ppb], rest[ppb:2 * ppb]
    o_ref, acc, run, qx = rest[2 * ppb:]
    p = pl.program_id(1)
    t, d = q_ref.shape
    page = k_refs[0].shape[-1]
    m = n_heads * t
    rowh = lax.broadcasted_iota(I32, (m, d), 0) // t
    colh = lax.broadcasted_iota(I32, (m, d), 1) // dh
    own = rowh == colh

    def upper_of(n):
        return (lax.broadcasted_iota(I32, (n, n), 0) > lax.broadcasted_iota(I32, (n, n), 1)).astype(BF16)

    nt = (((1,), (1,)), ((), ()))

    def step(kc, vc, mask, keys_on_lanes):
        if keys_on_lanes:
            z = jnp.dot(qx[...], kc, preferred_element_type=F32) * scale
        else:
            z = lax.dot_general(qx[...], kc, nt, preferred_element_type=F32) * scale
        a, new_run = _sb_block(z, mask, upper_of(z.shape[1]), run[...])
        if keys_on_lanes:
            acc[...] += lax.dot_general(a, vc, nt, preferred_element_type=F32)
        else:
            acc[...] += jnp.dot(a, vc, preferred_element_type=F32)
        run[...] = new_run

    @pl.when(p == 0)
    def _():
        qt = jnp.concatenate([q_ref[...]] * n_heads, axis=0)
        qx[...] = jnp.where(own, qt, jnp.zeros_like(qt))
        acc[...] = jnp.zeros_like(acc)
        run[...] = jnp.zeros_like(run)
        pad = jnp.zeros((page - t, d), BF16)
        kc = jnp.concatenate([kn_ref[...].astype(BF16), pad], axis=0)
        vc = jnp.concatenate([vn_ref[...].astype(BF16), pad], axis=0)
        key = lax.broadcasted_iota(I32, (m, page), 1)
        qpos = lax.broadcasted_iota(I32, (m, page), 0) % t
        step(kc, vc, key < qpos, False)

    def columns(ref):
        return ref[0, 0].reshape(d, page).astype(BF16)

    kc = jnp.concatenate([columns(k_refs[r]) for r in reversed(range(ppb))], axis=1)
    vc = jnp.concatenate([columns(v_refs[r]) for r in reversed(range(ppb))], axis=1)
    step(kc, vc, None, True)

    @pl.when(p == pl.num_programs(1) - 1)
    def _():
        sel = jnp.where(own, acc[...], 0.0).reshape(n_heads, t, d)
        o_ref[...] = jnp.sum(sel, axis=0).astype(BF16)


def _attn_sample(q, k_new, v_new, cache_k, cache_v, layer, page_table):
    n, d = q.shape
    db, n_pages = page_table.shape
    t = n // db
    _, _, page, n_heads, dh = cache_k.shape
    ppb = _pick(n_pages, (8, 4, 2, 1))
    steps = n_pages // ppb
    assert t <= page and n_heads * dh == d
    cache_k = jnp.transpose(cache_k, (0, 1, 3, 4, 2))
    cache_v = jnp.transpose(cache_v, (0, 1, 3, 4, 2))

    def page_spec(r):
        return pl.BlockSpec((1, 1, n_heads, dh, page),
                            lambda b, p, pt: (layer, pt[b, n_pages - 1 - (p * ppb + r)], 0, 0, 0))

    tok = pl.BlockSpec((t, d), lambda b, p, pt: (b, 0))
    grid_spec = pltpu.PrefetchScalarGridSpec(
        num_scalar_prefetch=1,
        grid=(db, steps),
        in_specs=[tok, tok, tok] + [page_spec(r) for r in range(ppb)] * 2,
        out_specs=tok,
        scratch_shapes=[pltpu.VMEM((n_heads * t, d), F32), pltpu.VMEM((n_heads * t, 1), F32),
                        pltpu.VMEM((n_heads * t, d), BF16)],
    )
    return pl.pallas_call(
        functools.partial(_attn_sample_kernel, scale=dh ** -0.5 * LOG2_E, dh=dh, n_heads=n_heads, ppb=ppb),
        out_shape=jax.ShapeDtypeStruct((n, d), BF16),
        grid_spec=grid_spec,
        compiler_params=_cparams(("arbitrary", "arbitrary")),
        name="attn_sample",
    )(page_table, q, k_new, v_new, *([cache_k] * ppb), *([cache_v] * ppb))


def _router_kernel(h_ref, w_ref, b_ref, idx_ref, gate_ref, rank_ref, cnt_ref, carry):
    i = pl.program_id(0)

    @pl.when(i == 0)
    def _():
        carry[...] = jnp.zeros_like(carry)

    logits = jnp.dot(h_ref[...], w_ref[...], preferred_element_type=F32,
                     precision=lax.Precision.HIGHEST) + b_ref[...]
    t, ep = logits.shape
    lane = lax.broadcasted_iota(I32, (t, ep), 1)
    left = logits
    chosen = jnp.zeros((t, ep), F32)
    vals, idxs = [], []
    for _ in range(TOP_K):
        top = jnp.max(left, axis=-1, keepdims=True)
        idx = jnp.min(jnp.where(left == top, lane, ep), axis=-1, keepdims=True)
        sel = lane == idx
        vals.append(top)
        idxs.append(idx)
        chosen = jnp.where(sel, 1.0, chosen)
        left = jnp.where(sel, -jnp.inf, left)
    exps = [jnp.exp(v - vals[0]) for v in vals]
    denom = exps[0]
    for e in exps[1:]:
        denom = denom + e
    lower = (lax.broadcasted_iota(I32, (t, t), 0) > lax.broadcasted_iota(I32, (t, t), 1)).astype(BF16)
    before = jnp.dot(lower, chosen.astype(BF16), preferred_element_type=F32) + carry[...]
    idx_o = jnp.zeros((t, ep), I32)
    gate_o = jnp.zeros((t, ep), F32)
    rank_o = jnp.zeros((t, ep), I32)
    for k in range(TOP_K):
        rank_k = jnp.sum(jnp.where(lane == idxs[k], before, 0.0), axis=-1, keepdims=True)
        idx_o = jnp.where(lane == k, idxs[k], idx_o)
        gate_o = jnp.where(lane == k, exps[k] / denom, gate_o)
        rank_o = jnp.where(lane == k, rank_k.astype(I32), rank_o)
    idx_ref[...] = idx_o[:, :TOP_K]
    gate_ref[...] = gate_o[:, :TOP_K]
    rank_ref[...] = rank_o[:, :TOP_K]
    carry[...] += jnp.sum(chosen, axis=0, keepdims=True)
    cnt_ref[...] = carry[...].astype(I32)


def _router(h, w_pad, b_pad):
    n, d = h.shape
    ep = w_pad.shape[1]
    t = _pick(n, (256, 128, 64, 32))
    small = pl.BlockSpec((t, TOP_K), lambda i: (i, 0))
    return pl.pallas_call(
        _router_kernel,
        out_shape=(jax.ShapeDtypeStruct((n, TOP_K), I32), jax.ShapeDtypeStruct((n, TOP_K), F32),
                   jax.ShapeDtypeStruct((n, TOP_K), I32), jax.ShapeDtypeStruct((1, ep), I32)),
        grid=(n // t,),
        in_specs=[pl.BlockSpec((t, d), lambda i: (i, 0)),
                  pl.BlockSpec((d, ep), lambda i: (0, 0)),
                  pl.BlockSpec((1, ep), lambda i: (0, 0))],
        out_specs=(small, small, small, pl.BlockSpec((1, ep), lambda i: (0, 0))),
        scratch_shapes=[pltpu.VMEM((1, ep), F32)],
        compiler_params=_cparams(("arbitrary",)),
        name="moe_router",
    )(h, w_pad, b_pad)


def _dispatch_kernel(pstart_ref, pad_ref, h_ref, idx_hbm, rank_hbm, xs_hbm,
                     idx_s0, idx_s1, rank_s0, rank_s1, rows0, rows1, zblk, sem_m, sem_r, sem_z, *, n_exp):
    i = pl.program_id(0)
    n_steps = pl.num_programs(0)
    td, d = h_ref.shape
    na = TOP_K * td
    bm = zblk.shape[0]
    idx_s, rank_s, rows = (idx_s0, idx_s1), (rank_s0, rank_s1), (rows0, rows1)

    def meta(step, slot):
        return (pltpu.make_async_copy(idx_hbm.at[pl.ds(step * td, td)], idx_s[slot], sem_m.at[0, slot]),
                pltpu.make_async_copy(rank_hbm.at[pl.ds(step * td, td)], rank_s[slot], sem_m.at[1, slot]))

    def wait_rows(slot):
        pltpu.make_async_copy(xs_hbm.at[pl.ds(0, na)], xs_hbm.at[pl.ds(0, na)], sem_r.at[slot]).wait()

    @pl.when(i == 0)
    def _():
        zblk[...] = jnp.zeros_like(zblk)
        total = pstart_ref[n_exp - 1] + pad_ref[n_exp - 1]
        n_tail = (xs_hbm.shape[0] - total) // bm

        def zero_copy(start):
            return pltpu.make_async_copy(zblk, xs_hbm.at[pl.ds(start, bm)], sem_z)

        def fill(e, _):
            @pl.when(pad_ref[e] > 0)
            def _():
                zero_copy(pstart_ref[e] + pad_ref[e] - bm).start()
            return 0

        def drain(e, _):
            @pl.when(pad_ref[e] > 0)
            def _():
                zero_copy(0).wait()
            return 0

        lax.fori_loop(0, n_exp, fill, 0)
        lax.fori_loop(0, n_tail, lambda j, c: (zero_copy(total + j * bm).start(), c)[1], 0)
        lax.fori_loop(0, n_exp, drain, 0)
        lax.fori_loop(0, n_tail, lambda j, c: (zero_copy(0).wait(), c)[1], 0)
        for c in meta(0, 0):
            c.start()

    def step(slot):
        @pl.when(i + 1 < n_steps)
        def _():
            for c in meta(i + 1, 1 - slot):
                c.start()

        for c in meta(i, slot):
            c.wait()
        rows[slot][...] = h_ref[...].reshape(td, 1, d)

        def row(t, _):
            for k in range(TOP_K):
                dst = pstart_ref[idx_s[slot][t, k]] + rank_s[slot][t, k]
                pltpu.make_async_copy(rows[slot].at[t], xs_hbm.at[dst], sem_r.at[slot]).start(priority=k % 2)
            return 0

        lax.fori_loop(0, td, row, 0)

        @pl.when(i > 0)
        def _():
            wait_rows(1 - slot)

        @pl.when(i == n_steps - 1)
        def _():
            wait_rows(slot)

    for slot in range(2):
        pl.when(i % 2 == slot)(functools.partial(step, slot))


def _dispatch(h, idx, rank, pstart, padded, cap, bm):
    n, d = h.shape
    n_exp = pstart.shape[0]
    td = _pick(n, (256, 128, 64, 32))
    grid_spec = pltpu.PrefetchScalarGridSpec(
        num_scalar_prefetch=2,
        grid=(n // td,),
        in_specs=[pl.BlockSpec((td, d), lambda i, ps, ct: (i, 0)),
                  pl.BlockSpec(memory_space=pl.ANY),
                  pl.BlockSpec(memory_space=pl.ANY)],
        out_specs=pl.BlockSpec(memory_space=pl.ANY),
        scratch_shapes=[pltpu.SMEM((td, TOP_K), I32)] * 4
        + [pltpu.VMEM((td, 1, d), F32), pltpu.VMEM((td, 1, d), F32), pltpu.VMEM((bm, 1, d), F32),
           pltpu.SemaphoreType.DMA((2, 2)), pltpu.SemaphoreType.DMA((2,)), pltpu.SemaphoreType.DMA],
    )
    return pl.pallas_call(
        functools.partial(_dispatch_kernel, n_exp=n_exp),
        out_shape=jax.ShapeDtypeStruct((cap, 1, d), F32),
        grid_spec=grid_spec,
        compiler_params=_cparams(("arbitrary",)),
        name="moe_dispatch",
    )(pstart, padded, h, idx, rank)


def _experts_kernel(be_ref, nvb_ref, xs_ref, w1_ref, b1_ref, w2_ref, b2_ref, ys_ref, w1b, w2b, x2d):
    i = pl.program_id(0)
    valid = i < nvb_ref[0]
    f, d = w2b.shape
    bm = x2d.shape[0]
    fresh = jnp.logical_or(i == 0, be_ref[i] != be_ref[jnp.maximum(i - 1, 0)])

    @pl.when(jnp.logical_and(valid, fresh))
    def _():
        w1b[...] = w1_ref[0, 0].astype(BF16)
        w2b[...] = w2_ref[0, 0].astype(BF16)

    @pl.when(valid)
    def _():
        x2d[...] = xs_ref[...].reshape(bm, d)
        h = jnp.dot(x2d[...].astype(BF16), w1b[...], preferred_element_type=F32) + b1_ref[0, 0]
        gate = jnp.minimum(h[:, :f], SWIGLU_LIMIT)
        up = jnp.clip(h[:, f:], -SWIGLU_LIMIT, SWIGLU_LIMIT)
        act = (up + 1.0) * gate * jax.nn.sigmoid(SWIGLU_ALPHA * gate)
        y = jnp.dot(act.astype(BF16), w2b[...], preferred_element_type=F32) + b2_ref[0, 0]
        ys_ref[...] = y.reshape(bm, 1, d)

    @pl.when(jnp.logical_not(valid))
    def _():
        ys_ref[...] = jnp.zeros_like(ys_ref)


def _experts(xs, block_e, n_valid, w1, b1, w2, b2, layer, n_blocks, bm):
    d = xs.shape[-1]
    _, e, _, f2 = w1.shape
    f = f2 // 2
    b1 = b1.reshape(-1, e, 1, f2)
    b2 = b2.reshape(-1, e, 1, d)
    grid_spec = pltpu.PrefetchScalarGridSpec(
        num_scalar_prefetch=2,
        grid=(n_blocks,),
        in_specs=[pl.BlockSpec((bm, 1, d), lambda i, be, nv: (jnp.minimum(i, nv[0] - 1), 0, 0)),
                  pl.BlockSpec((1, 1, d, f2), lambda i, be, nv: (layer, be[i], 0, 0)),
                  pl.BlockSpec((1, 1, 1, f2), lambda i, be, nv: (layer, be[i], 0, 0)),
                  pl.BlockSpec((1, 1, f, d), lambda i, be, nv: (layer, be[i], 0, 0)),
                  pl.BlockSpec((1, 1, 1, d), lambda i, be, nv: (layer, be[i], 0, 0))],
        out_specs=pl.BlockSpec((bm, 1, d), lambda i, be, nv: (i, 0, 0)),
        scratch_shapes=[pltpu.VMEM((d, f2), BF16), pltpu.VMEM((f, d), BF16), pltpu.VMEM((bm, d), F32)],
    )
    return pl.pallas_call(
        _experts_kernel,
        out_shape=jax.ShapeDtypeStruct((n_blocks * bm, 1, d), F32),
        grid_spec=grid_spec,
        compiler_params=_cparams(("arbitrary",)),
        name="moe_experts",
    )(block_e, n_valid, xs, w1, b1, w2, b2)


def _combine_kernel(pstart_ref, h_ref, gate_ref, idx_hbm, rank_hbm, ys_hbm, g_ref, b_ref, o_ref,
                    idx_s0, idx_s1, rank_s0, rank_s1, buf0, buf1, y2d, sem_m, sem_r, *, alpha, blk_off):
    i = pl.program_id(0)
    n_steps = pl.num_programs(0)
    tf, d = h_ref.shape
    idx_s, rank_s, buf = (idx_s0, idx_s1), (rank_s0, rank_s1), (buf0, buf1)

    def meta(step, slot):
        rows = pl.ds((step + blk_off) * tf, tf)
        return (pltpu.make_async_copy(idx_hbm.at[rows], idx_s[slot], sem_m.at[0, slot]),
                pltpu.make_async_copy(rank_hbm.at[rows], rank_s[slot], sem_m.at[1, slot]))

    def gather(slot):
        def row(t, _):
            for k in range(TOP_K):
                src = pstart_ref[idx_s[slot][t, k]] + rank_s[slot][t, k]
                pltpu.make_async_copy(ys_hbm.at[src], buf[slot].at[k * tf + t], sem_r.at[slot]).start()
            return 0

        lax.fori_loop(0, tf, row, 0)

    @pl.when(i == 0)
    def _():
        for c in meta(0, 0):
            c.start()
        for c in meta(0, 0):
            c.wait()
        gather(0)

        @pl.when(n_steps > 1)
        def _():
            for c in meta(1, 1):
                c.start()

    def step(slot):
        @pl.when(i + 2 < n_steps)
        def _():
            for c in meta(i + 2, slot):
                c.start()

        @pl.when(i + 1 < n_steps)
        def _():
            for c in meta(i + 1, 1 - slot):
                c.wait()
            gather(1 - slot)

        pltpu.make_async_copy(buf[slot], buf[slot], sem_r.at[slot]).wait()
        y = alpha * h_ref[...]
        gates = gate_ref[...]
        for k in range(TOP_K):
            y2d[...] = buf[slot][pl.ds(k * tf, tf)].reshape(tf, d)
            y = y + gates[:, k:k + 1] * y2d[...]
        o_ref[...] = _layer_norm(y, g_ref[...], b_ref[...])

    for slot in range(2):
        pl.when(i % 2 == slot)(functools.partial(step, slot))


def _combine(h, gates, idx, rank, ys, pstart, g, beta, alpha, n, row_off):
    d = h.shape[1]
    tf = _pick(math.gcd(n, row_off) if row_off else n, (256, 128, 64, 32))
    bo = row_off // tf
    grid_spec = pltpu.PrefetchScalarGridSpec(
        num_scalar_prefetch=1,
        grid=(n // tf,),
        in_specs=[pl.BlockSpec((tf, d), lambda i, ps: (i + bo, 0)),
                  pl.BlockSpec((tf, TOP_K), lambda i, ps: (i + bo, 0)),
                  pl.BlockSpec(memory_space=pl.ANY),
                  pl.BlockSpec(memory_space=pl.ANY),
                  pl.BlockSpec(memory_space=pl.ANY),
                  pl.BlockSpec((1, d), lambda i, ps: (0, 0)),
                  pl.BlockSpec((1, d), lambda i, ps: (0, 0))],
        out_specs=pl.BlockSpec((tf, d), lambda i, ps: (i, 0)),
        scratch_shapes=[pltpu.SMEM((tf, TOP_K), I32)] * 4
        + [pltpu.VMEM((TOP_K * tf, 1, d), F32), pltpu.VMEM((TOP_K * tf, 1, d), F32), pltpu.VMEM((tf, d), F32),
           pltpu.SemaphoreType.DMA((2, 2)), pltpu.SemaphoreType.DMA((2,))],
    )
    return pl.pallas_call(
        functools.partial(_combine_kernel, alpha=alpha, blk_off=bo),
        out_shape=jax.ShapeDtypeStruct((n, d), F32),
        grid_spec=grid_spec,
        compiler_params=_cparams(("arbitrary",)),
        name="moe_combine_ln",
    )(pstart, h, gates, idx, rank, ys, g.reshape(1, -1), beta.reshape(1, -1))


def _moe_ln(h, w_router, b_router, w1, b1, w2, b2, layer, g, beta, alpha, splits):
    n, d = h.shape
    n_exp = w_router.shape[1]
    ep = -(-n_exp // LANES) * LANES
    bm = EXPERT_BLOCK
    w_pad = jnp.zeros((d, ep), F32).at[:, :n_exp].set(w_router)
    b_pad = jnp.full((1, ep), NEG_BIG, F32).at[0, :n_exp].set(b_router)
    idx, gates, rank, counts = _router(h, w_pad, b_pad)
    counts = counts[0, :n_exp]
    padded = (counts + bm - 1) // bm * bm
    ends = jnp.cumsum(padded)
    pstart = (ends - padded).astype(I32)
    n_blocks = -(-(n * TOP_K + n_exp * (bm - 1)) // bm)
    starts = jnp.arange(n_blocks, dtype=I32) * bm
    block_e = jnp.minimum(jnp.sum(starts[:, None] >= ends[None, :], axis=1), n_exp - 1).astype(I32)
    n_valid = (ends[-1:] // bm).astype(I32)
    xs = _dispatch(h, idx, rank, pstart, padded.astype(I32), n_blocks * bm, bm)
    ys = _experts(xs, block_e, n_valid, w1, b1, w2, b2, layer, n_blocks, bm)
    return [_combine(h, gates, idx, rank, ys, pstart, g, beta, alpha, rows, off)
            for off, rows in splits]


def kernel(x_prompt, x_sample, state_conv, cache_k, cache_v, page_table, w_pw1, b_pw1, dw_w, dw_b, cn_g, cn_b, w_pw2, b_pw2, w_qkv, b_qkv, w_o, ln_mix_g, ln_mix_b, ln_ffn_g, ln_ffn_b, w_router, b_router, w_e1, b_e1, w_e2, b_e2):
    b, s, d = x_prompt.shape
    db, t, _ = x_sample.shape
    depth = ln_mix_g.shape[0]
    n_heads, dh = cache_k.shape[3], cache_k.shape[4]
    width = dw_w.shape[1]
    n_p, n_s = b * s, db * t
    n = n_p + n_s
    alpha = (2 * depth) ** 0.25
    hist = -(-(width - 1) // SUBLANES) * SUBLANES

    xp, xs_ = x_prompt.reshape(n_p, d), x_sample.reshape(n_s, d)
    h = None
    conv_p, conv_s, k_p, v_p, k_s, v_s = [], [], [], [], [], []
    for i in range(depth):
        j = i // 2
        last = i == depth - 1
        if i % 2 == 0:
            w1_bf = w_pw1[j].astype(BF16)
            up = _mm_glu(xp if h is None else h[:n_p], w1_bf, b_pw1[j])
            us = _mm_glu(xs_ if h is None else h[n_p:], w1_bf, b_pw1[j])
            state = jnp.pad(state_conv[j].astype(F32), ((0, 0), (hist - (width - 1), 0), (0, 0)))
            yp, cp = _conv(up.reshape(b, s, d), None, dw_w[j], dw_b[j], cn_g[j], cn_b[j])
            ys, cs = _conv(us.reshape(db, t, d), state, dw_w[j], dw_b[j], cn_g[j], cn_b[j])
            conv_p.append(cp)
            conv_s.append(cs)
            w2_bf = w_pw2[j].astype(BF16)
            res = [(xp, 0), (xs_, 0)] if h is None else [(h, 0), (h, n_p)]
            segs = [(yp.reshape(n_p, d), *res[0]), (ys.reshape(n_s, d), *res[1])]
            hm = _mm_res_ln(segs, w2_bf, b_pw2[j], ln_mix_g[i], ln_mix_b[i], alpha)
        else:
            wq_bf = w_qkv[j].astype(BF16)
            qp, kp_t, vp_t = _mm_qkv_t(h, wq_bf, b_qkv[j], b, s)
            qs, ks, vs = _mm_qkv(h, wq_bf, b_qkv[j], n_s, n_p)
            op = _attn_prompt(qp, kp_t, vp_t, dh)
            os_ = _attn_sample(qs, ks, vs, cache_k, cache_v, j, page_table)
            k_p.append(jnp.transpose(kp_t.reshape(b, n_heads, dh, s), (0, 3, 1, 2)))
            v_p.append(jnp.transpose(vp_t.reshape(b, n_heads, dh, s), (0, 3, 1, 2)))
            k_s.append(ks.reshape(db, t, n_heads, dh))
            v_s.append(vs.reshape(db, t, n_heads, dh))
            hm = _mm_res_ln([(op, h, 0), (os_, h, n_p)], w_o[j].astype(BF16), jnp.zeros((d,), F32),
                            ln_mix_g[i], ln_mix_b[i], alpha)
        splits = [(0, n_p), (n_p, n_s)] if last else [(0, n)]
        outs = _moe_ln(hm, w_router[i], b_router[i], w_e1, b_e1, w_e2, b_e2, i,
                       ln_ffn_g[i], ln_ffn_b[i], alpha, splits)
        if last:
            y_p, y_s = outs
        else:
            h = outs[0]
    return (y_p.reshape(b, s, d), y_s.reshape(db, t, d), jnp.stack(conv_p), jnp.stack(conv_s),
            jnp.stack(k_p), jnp.stack(v_p), jnp.stack(k_s), jnp.stack(v_s))
```

```python
import functools
import math

import jax
import jax.numpy as jnp
from jax import lax
from jax.experimental import pallas as pl
from jax.experimental.pallas import tpu as pltpu

F32 = jnp.float32
BF16 = jnp.bfloat16
I32 = jnp.int32

TOP_K = 4
SWIGLU_LIMIT = 7.0
SWIGLU_ALPHA = 1.702
LN_EPS = 1e-5
LANES = 128
SUBLANES = 8
VMEM_LIMIT = 56 * 1024 * 1024
EXPERT_BLOCK = 512
NEG_BIG = -1e30
LOG2_E = math.log2(math.e)


def _pick(n, cands):
    for c in cands:
        if n % c == 0:
            return c
    raise ValueError(f"no tile in {cands} divides {n}")


def _cparams(sem):
    return pltpu.CompilerParams(dimension_semantics=sem, vmem_limit_bytes=VMEM_LIMIT)


def _layer_norm(x, g, b):
    mu = jnp.mean(x, axis=-1, keepdims=True)
    xc = x - mu
    var = jnp.mean(xc * xc, axis=-1, keepdims=True)
    return xc * lax.rsqrt(var + LN_EPS) * g + b


def _mm_glu_kernel(x_ref, w_ref, b_ref, u_ref):
    d = u_ref.shape[-1]
    h = jnp.dot(x_ref[...].astype(BF16), w_ref[...], preferred_element_type=F32) + b_ref[...]
    u_ref[...] = h[:, :d] * jax.nn.sigmoid(h[:, d:])


def _mm_glu(x, w_bf, b):
    n, d = x.shape
    tm = _pick(n, (512, 256, 128, 64, 32, 16, 8))
    return pl.pallas_call(
        _mm_glu_kernel,
        out_shape=jax.ShapeDtypeStruct((n, d), F32),
        grid=(n // tm,),
        in_specs=[pl.BlockSpec((tm, d), lambda i: (i, 0)),
                  pl.BlockSpec((d, 2 * d), lambda i: (0, 0)),
                  pl.BlockSpec((1, 2 * d), lambda i: (0, 0))],
        out_specs=pl.BlockSpec((tm, d), lambda i: (i, 0)),
        compiler_params=_cparams(("arbitrary",)),
        name="mm_glu",
    )(x, w_bf, b.reshape(1, -1))


def _mm_res_ln_kernel(w_ref, b_ref, g_ref, beta_ref, *rest, alpha, bounds):
    o_ref = rest[-1]
    i = pl.program_id(0)
    for s, (lo, hi) in enumerate(bounds):
        a_ref, res_ref = rest[2 * s], rest[2 * s + 1]

        @pl.when(jnp.logical_and(i >= lo, i < hi))
        def _(a_ref=a_ref, res_ref=res_ref):
            m = jnp.dot(a_ref[...].astype(BF16), w_ref[...], preferred_element_type=F32) + b_ref[...]
            o_ref[...] = _layer_norm(alpha * res_ref[...] + m, g_ref[...], beta_ref[...])


def _mm_res_ln(segs, w_bf, b, g, beta, alpha):
    k, d = w_bf.shape
    tm = functools.reduce(math.gcd, [x for a, _, off in segs for x in (a.shape[0], off)])
    tm = _pick(tm, (512, 256, 128, 64, 32, 16, 8))
    vec = pl.BlockSpec((1, d), lambda i: (0, 0))
    in_specs = [pl.BlockSpec((k, d), lambda i: (0, 0)), vec, vec, vec]
    args = [w_bf, b.reshape(1, -1), g.reshape(1, -1), beta.reshape(1, -1)]
    bounds, lo = [], 0
    for a, res, off in segs:
        nt = a.shape[0] // tm
        in_specs.append(pl.BlockSpec((tm, k), lambda i, lo=lo, nt=nt: (jnp.clip(i - lo, 0, nt - 1), 0)))
        in_specs.append(pl.BlockSpec((tm, d), lambda i, lo=lo, nt=nt, so=off // tm:
                                     (jnp.clip(i - lo, 0, nt - 1) + so, 0)))
        args += [a, res]
        bounds.append((lo, lo + nt))
        lo += nt
    return pl.pallas_call(
        functools.partial(_mm_res_ln_kernel, alpha=alpha, bounds=tuple(bounds)),
        out_shape=jax.ShapeDtypeStruct((lo * tm, d), F32),
        grid=(lo,),
        in_specs=in_specs,
        out_specs=pl.BlockSpec((tm, d), lambda i: (i, 0)),
        compiler_params=_cparams(("arbitrary",)),
        name="mm_res_ln",
    )(*args)


def _mm_qkv_kernel(h_ref, w_ref, b_ref, q_ref, k_ref, v_ref):
    d = k_ref.shape[-1]
    r = jnp.dot(h_ref[...].astype(BF16), w_ref[...], preferred_element_type=F32) + b_ref[...]
    q_ref[...] = r[:, :d].astype(BF16)
    k_ref[...] = r[:, d:2 * d]
    v_ref[...] = r[:, 2 * d:]


def _mm_qkv_t_kernel(h_ref, wq_ref, bq_ref, wkv_ref, bkv_ref, q_ref, kt_ref, vt_ref):
    d = q_ref.shape[-1]
    hb = h_ref[...].astype(BF16)
    q_ref[...] = (jnp.dot(hb, wq_ref[...], preferred_element_type=F32) + bq_ref[...]).astype(BF16)
    kv = lax.dot_general(wkv_ref[...], hb, (((1,), (1,)), ((), ())), preferred_element_type=F32) + bkv_ref[...]
    kt_ref[0] = kv[:d]
    vt_ref[0] = kv[d:]


def _mm_qkv_t(h, w_bf, b, batch, seq):
    d = h.shape[1]
    tm = _pick(seq, (512, 256, 128))
    nt = seq // tm
    wq, wkv_t = w_bf[:, :d], w_bf[:, d:].T
    bq, bkv = b[:d].reshape(1, d), b[d:].reshape(2 * d, 1)
    const = lambda shape: pl.BlockSpec(shape, lambda bi, i: (0, 0))
    return pl.pallas_call(
        _mm_qkv_t_kernel,
        out_shape=(jax.ShapeDtypeStruct((batch * seq, d), BF16),
                   jax.ShapeDtypeStruct((batch, d, seq), F32),
                   jax.ShapeDtypeStruct((batch, d, seq), F32)),
        grid=(batch, nt),
        in_specs=[pl.BlockSpec((tm, d), lambda bi, i: (bi * nt + i, 0)),
                  const((d, d)), const((1, d)), const((2 * d, d)), const((2 * d, 1))],
        out_specs=(pl.BlockSpec((tm, d), lambda bi, i: (bi * nt + i, 0)),
                   pl.BlockSpec((1, d, tm), lambda bi, i: (bi, 0, i)),
                   pl.BlockSpec((1, d, tm), lambda bi, i: (bi, 0, i))),
        compiler_params=_cparams(("arbitrary", "arbitrary")),
        name="mm_qkv_t",
    )(h, wq, bq, wkv_t, bkv)


def _mm_qkv(h, w_bf, b, n, row_off):
    d = h.shape[1]
    tm = _pick(math.gcd(n, row_off) if row_off else n, (512, 256, 128, 64, 32, 16, 8))
    ro = row_off // tm
    return pl.pallas_call(
        _mm_qkv_kernel,
        out_shape=(jax.ShapeDtypeStruct((n, d), BF16),
                   jax.ShapeDtypeStruct((n, d), F32),
                   jax.ShapeDtypeStruct((n, d), F32)),
        grid=(n // tm,),
        in_specs=[pl.BlockSpec((tm, d), lambda i: (i + ro, 0)),
                  pl.BlockSpec((d, 3 * d), lambda i: (0, 0)),
                  pl.BlockSpec((1, 3 * d), lambda i: (0, 0))],
        out_specs=(pl.BlockSpec((tm, d), lambda i: (i, 0)),) * 3,
        compiler_params=_cparams(("arbitrary",)),
        name="mm_qkv",
    )(h, w_bf, b.reshape(1, -1))


def _conv_kernel(*refs, width, hist, has_state, n_tiles):
    if has_state:
        u_ref, st_ref, w_ref, wb_ref, g_ref, b_ref, y_ref, cs_ref, buf, conv, taps = refs
    else:
        u_ref, w_ref, wb_ref, g_ref, b_ref, y_ref, cs_ref, buf, conv, taps = refs
    t = u_ref.shape[1]
    d = u_ref.shape[2]
    ti = pl.program_id(1)
    n_shift = buf.shape[0]
    span = hist + t - n_shift

    @pl.when(ti == 0)
    def _():
        if has_state:
            buf[0, 0:hist, :] = st_ref[0]
        else:
            buf[0, 0:hist, :] = jnp.zeros((hist, d), F32)

    buf[0, hist:hist + t, :] = u_ref[0]
    for r in range(1, n_shift):
        buf[r, 0:span, :] = buf[0, r:r + span, :]
    lo = hist - (width - 1)
    rows = min(t, 2 * SUBLANES)

    @pl.when(jnp.logical_and(pl.program_id(0) == 0, ti == 0))
    def _():
        for k in range(width):
            taps[k] = jnp.zeros((rows, d), F32) + w_ref[k:k + 1, :]

    def chunk(c, _):
        base = pl.multiple_of(c * rows, rows)
        acc = jnp.zeros((rows, d), F32) + wb_ref[...]
        for k in range(width):
            o = lo + k
            win = buf[o % n_shift, pl.ds(base + (o // n_shift) * n_shift, rows), :]
            acc = acc + win * taps[k]
        conv[pl.ds(base, rows), :] = acc
        return 0

    lax.fori_loop(0, t // rows, chunk, 0)
    y = _layer_norm(conv[...], g_ref[...], b_ref[...])
    y_ref[0] = (y * jax.nn.sigmoid(y)).astype(BF16)
    cs_ref[0] = buf[0, t + lo:t + hist, :]
    if n_tiles > 1:
        buf[0, 0:hist, :] = buf[0, t:t + hist, :]


def _conv(u, state, dw_w, dw_b, cn_g, cn_b):
    b, s, d = u.shape
    width = dw_w.shape[0]
    hist = -(-(width - 1) // SUBLANES) * SUBLANES
    t = _pick(s, (512, 256, 128, 64, 32, 16, 8))
    n_tiles = s // t
    assert n_tiles == 1 or t >= hist
    vec = pl.BlockSpec((1, d), lambda i, j: (0, 0))
    in_specs = [pl.BlockSpec((1, t, d), lambda i, j: (i, j, 0))]
    args = [u]
    if state is not None:
        in_specs.append(pl.BlockSpec((1, hist, d), lambda i, j: (i, 0, 0)))
        args.append(state)
    in_specs += [pl.BlockSpec((width, d), lambda i, j: (0, 0)), vec, vec, vec]
    args += [dw_w, dw_b.reshape(1, -1), cn_g.reshape(1, -1), cn_b.reshape(1, -1)]
    return pl.pallas_call(
        functools.partial(_conv_kernel, width=width, hist=hist, has_state=state is not None,
                          n_tiles=n_tiles),
        out_shape=(jax.ShapeDtypeStruct((b, s, d), BF16),
                   jax.ShapeDtypeStruct((b, width - 1, d), F32)),
        grid=(b, n_tiles),
        in_specs=in_specs,
        out_specs=(pl.BlockSpec((1, t, d), lambda i, j: (i, j, 0)),
                   pl.BlockSpec((1, width - 1, d), lambda i, j: (i, 0, 0))),
        scratch_shapes=[pltpu.VMEM((SUBLANES, hist + t, d), F32), pltpu.VMEM((t, d), F32),
                        pltpu.VMEM((width, min(t, 2 * SUBLANES), d), F32)],
        compiler_params=_cparams(("arbitrary", "arbitrary")),
        name="conv_ln_swish",
    )(*args)


def _sb_block(z, mask, upper, run):
    neg_abs = lax.bitcast_convert_type(lax.bitcast_convert_type(z, jnp.uint32) | jnp.uint32(0x80000000), F32)
    drop = jnp.maximum(z, 0.0) + jnp.log2(1.0 + jnp.exp2(neg_abs))
    log_beta = z - drop
    if mask is not None:
        drop = jnp.where(mask, drop, 0.0)
    later = jnp.dot(drop.astype(BF16), upper, preferred_element_type=F32)
    a = jnp.exp2(log_beta - later - run)
    if mask is not None:
        a = jnp.where(mask, a, 0.0)
    return a.astype(BF16), run + jnp.sum(drop, axis=1, keepdims=True)


def _attn_prompt_kernel(q_ref, k_ref, v_ref, o_ref, kb, vb, *, scale, dh):
    qi = pl.program_id(2)
    blk, width = q_ref.shape
    heads = width // dh

    @pl.when(qi == 0)
    def _():
        for jj in range(kb.shape[0]):
            kb[jj] = k_ref[0, :, jj * blk:(jj + 1) * blk].astype(BF16)
            vb[jj] = v_ref[0, :, jj * blk:(jj + 1) * blk].astype(BF16)

    q = q_ref[...]
    lane = lax.broadcasted_iota(I32, (blk, width), 1)
    row = lax.broadcasted_iota(I32, (blk, blk), 0)
    col = lax.broadcasted_iota(I32, (blk, blk), 1)
    upper = (row > col).astype(BF16)
    causal = col < row
    q_heads = [jnp.where((lane >= h * dh) & (lane < (h + 1) * dh), q, jnp.zeros_like(q))
               for h in range(heads)]

    def block(j, mask, carry):
        ks = kb[j]
        vs = vb[j]
        out = []
        for h in range(heads):
            acc, run = carry[h]
            z = jnp.dot(q_heads[h], ks, preferred_element_type=F32) * scale
            a, run = _sb_block(z, mask, upper, run)
            pv = lax.dot_general(a, vs, (((1,), (1,)), ((), ())), preferred_element_type=F32)
            out.append((acc + pv, run))
        return tuple(out)

    init = tuple((jnp.zeros((blk, width), F32), jnp.zeros((blk, 1), F32)) for _ in range(heads))
    carry = block(qi, causal, init)
    carry = lax.fori_loop(0, qi, lambda jj, c: block(qi - 1 - jj, None, c), carry)
    out = carry[0][0]
    for h in range(1, heads):
        out = jnp.where(lane >= h * dh, carry[h][0], out)
    o_ref[...] = out.astype(BF16)


def _attn_prompt(q, k_t, v_t, dh):
    n, d = q.shape
    batch, _, seq = k_t.shape
    width = LANES
    assert width % dh == 0 and d % width == 0
    blk = _pick(seq, (512, 256, 128))
    nq = seq // blk
    return pl.pallas_call(
        functools.partial(_attn_prompt_kernel, scale=dh ** -0.5 * LOG2_E, dh=dh),
        out_shape=jax.ShapeDtypeStruct((n, d), BF16),
        grid=(batch, d // width, nq),
        in_specs=[pl.BlockSpec((blk, width), lambda b, h, i: (b * nq + i, h)),
                  pl.BlockSpec((1, width, seq), lambda b, h, i: (b, h, 0)),
                  pl.BlockSpec((1, width, seq), lambda b, h, i: (b, h, 0))],
        out_specs=pl.BlockSpec((blk, width), lambda b, h, i: (b * nq + i, h)),
        scratch_shapes=[pltpu.VMEM((nq, width, blk), BF16), pltpu.VMEM((nq, width, blk), BF16)],
        compiler_params=_cparams(("arbitrary", "arbitrary", "arbitrary")),
        name="attn_prompt",
    )(q, k_t, v_t)


def _attn_sample_kernel(pt_ref, q_ref, kn_ref, vn_ref, *rest, scale, dh, n_heads, ppb):
    del pt_ref
    k_refs, v_refs = rest[:ppb], rest[ppb:2 * ppb]
    o_ref, acc, run, qx = rest[2 * ppb:]
    p = pl.program_id(1)
    t, d = q_ref.shape
    page = k_refs[0].shape[-1]
    m = n_heads * t
    rowh = lax.broadcasted_iota(I32, (m, d), 0) // t
    colh = lax.broadcasted_iota(I32, (m, d), 1) // dh
    own = rowh == colh

    def upper_of(n):
        return (lax.broadcasted_iota(I32, (n, n), 0) > lax.broadcasted_iota(I32, (n, n), 1)).astype(BF16)

    nt = (((1,), (1,)), ((), ()))

    def step(kc, vc, mask, keys_on_lanes):
        if keys_on_lanes:
            z = jnp.dot(qx[...], kc, preferred_element_type=F32) * scale
        else:
            z = lax.dot_general(qx[...], kc, nt, preferred_element_type=F32) * scale
        a, new_run = _sb_block(z, mask, upper_of(z.shape[1]), run[...])
        if keys_on_lanes:
            acc[...] += lax.dot_general(a, vc, nt, preferred_element_type=F32)
        else:
            acc[...] += jnp.dot(a, vc, preferred_element_type=F32)
        run[...] = new_run

    @pl.when(p == 0)
    def _():
        qt = jnp.concatenate([q_ref[...]] * n_heads, axis=0)
        qx[...] = jnp.where(own, qt, jnp.zeros_like(qt))
        acc[...] = jnp.zeros_like(acc)
        run[...] = jnp.zeros_like(run)
        pad = jnp.zeros((page - t, d), BF16)
        kc = jnp.concatenate([kn_ref[...].astype(BF16), pad], axis=0)
        vc = jnp.concatenate([vn_ref[...].astype(BF16), pad], axis=0)
        key = lax.broadcasted_iota(I32, (m, page), 1)
        qpos = lax.broadcasted_iota(I32, (m, page), 0) % t
        step(kc, vc, key < qpos, False)

    def columns(ref):
        return ref[0, 0].reshape(d, page).astype(BF16)

    kc = jnp.concatenate([columns(k_refs[r]) for r in reversed(range(ppb))], axis=1)
    vc = jnp.concatenate([columns(v_refs[r]) for r in reversed(range(ppb))], axis=1)
    step(kc, vc, None, True)

    @pl.when(p == pl.num_programs(1) - 1)
    def _():
        sel = jnp.where(own, acc[...], 0.0).reshape(n_heads, t, d)
        o_ref[...] = jnp.sum(sel, axis=0).astype(BF16)


def _attn_sample(q, k_new, v_new, cache_k, cache_v, layer, page_table):
    n, d = q.shape
    db, n_pages = page_table.shape
    t = n // db
    _, _, page, n_heads, dh = cache_k.shape
    ppb = _pick(n_pages, (8, 4, 2, 1))
    steps = n_pages // ppb
    assert t <= page and n_heads * dh == d
    cache_k = jnp.transpose(cache_k, (0, 1, 3, 4, 2))
    cache_v = jnp.transpose(cache_v, (0, 1, 3, 4, 2))

    def page_spec(r):
        return pl.BlockSpec((1, 1, n_heads, dh, page),
                            lambda b, p, pt: (layer, pt[b, n_pages - 1 - (p * ppb + r)], 0, 0, 0))

    tok = pl.BlockSpec((t, d), lambda b, p, pt: (b, 0))
    grid_spec = pltpu.PrefetchScalarGridSpec(
        num_scalar_prefetch=1,
        grid=(db, steps),
        in_specs=[tok, tok, tok] + [page_spec(r) for r in range(ppb)] * 2,
        out_specs=tok,
        scratch_shapes=[pltpu.VMEM((n_heads * t, d), F32), pltpu.VMEM((n_heads * t, 1), F32),
                        pltpu.VMEM((n_heads * t, d), BF16)],
    )
    return pl.pallas_call(
        functools.partial(_attn_sample_kernel, scale=dh ** -0.5 * LOG2_E, dh=dh, n_heads=n_heads, ppb=ppb),
        out_shape=jax.ShapeDtypeStruct((n, d), BF16),
        grid_spec=grid_spec,
        compiler_params=_cparams(("arbitrary", "arbitrary")),
        name="attn_sample",
    )(page_table, q, k_new, v_new, *([cache_k] * ppb), *([cache_v] * ppb))


def _router_kernel(h_ref, w_ref, b_ref, idx_ref, gate_ref, rank_ref, cnt_ref, carry):
    i = pl.program_id(0)

    @pl.when(i == 0)
    def _():
        carry[...] = jnp.zeros_like(carry)

    logits = jnp.dot(h_ref[...], w_ref[...], preferred_element_type=F32,
                     precision=lax.Precision.HIGHEST) + b_ref[...]
    t, ep = logits.shape
    lane = lax.broadcasted_iota(I32, (t, ep), 1)
    left = logits
    chosen = jnp.zeros((t, ep), F32)
    vals, idxs = [], []
    for _ in range(TOP_K):
        top = jnp.max(left, axis=-1, keepdims=True)
        idx = jnp.min(jnp.where(left == top, lane, ep), axis=-1, keepdims=True)
        sel = lane == idx
        vals.append(top)
        idxs.append(idx)
        chosen = jnp.where(sel, 1.0, chosen)
        left = jnp.where(sel, -jnp.inf, left)
    exps = [jnp.exp(v - vals[0]) for v in vals]
    denom = exps[0]
    for e in exps[1:]:
        denom = denom + e
    lower = (lax.broadcasted_iota(I32, (t, t), 0) > lax.broadcasted_iota(I32, (t, t), 1)).astype(BF16)
    before = jnp.dot(lower, chosen.astype(BF16), preferred_element_type=F32) + carry[...]
    idx_o = jnp.zeros((t, ep), I32)
    gate_o = jnp.zeros((t, ep), F32)
    rank_o = jnp.zeros((t, ep), I32)
    for k in range(TOP_K):
        rank_k = jnp.sum(jnp.where(lane == idxs[k], before, 0.0), axis=-1, keepdims=True)
        idx_o = jnp.where(lane == k, idxs[k], idx_o)
        gate_o = jnp.where(lane == k, exps[k] / denom, gate_o)
        rank_o = jnp.where(lane == k, rank_k.astype(I32), rank_o)
    idx_ref[...] = idx_o[:, :TOP_K]
    gate_ref[...] = gate_o[:, :TOP_K]
    rank_ref[...] = rank_o[:, :TOP_K]
    carry[...] += jnp.sum(chosen, axis=0, keepdims=True)
    cnt_ref[...] = carry[...].astype(I32)


def _router(h, w_pad, b_pad):
    n, d = h.shape
    ep = w_pad.shape[1]
    t = _pick(n, (256, 128, 64, 32))
    small = pl.BlockSpec((t, TOP_K), lambda i: (i, 0))
    return pl.pallas_call(
        _router_kernel,
        out_shape=(jax.ShapeDtypeStruct((n, TOP_K), I32), jax.ShapeDtypeStruct((n, TOP_K), F32),
                   jax.ShapeDtypeStruct((n, TOP_K), I32), jax.ShapeDtypeStruct((1, ep), I32)),
        grid=(n // t,),
        in_specs=[pl.BlockSpec((t, d), lambda i: (i, 0)),
                  pl.BlockSpec((d, ep), lambda i: (0, 0)),
                  pl.BlockSpec((1, ep), lambda i: (0, 0))],
        out_specs=(small, small, small, pl.BlockSpec((1, ep), lambda i: (0, 0))),
        scratch_shapes=[pltpu.VMEM((1, ep), F32)],
        compiler_params=_cparams(("arbitrary",)),
        name="moe_router",
    )(h, w_pad, b_pad)


def _dispatch_kernel(pstart_ref, pad_ref, h_ref, idx_hbm, rank_hbm, xs_hbm,
                     idx_s0, idx_s1, rank_s0, rank_s1, rows0, rows1, zblk, sem_m, sem_r, sem_z, *, n_exp):
    i = pl.program_id(0)
    n_steps = pl.num_programs(0)
    td, d = h_ref.shape
    na = TOP_K * td
    bm = zblk.shape[0]
    idx_s, rank_s, rows = (idx_s0, idx_s1), (rank_s0, rank_s1), (rows0, rows1)

    def meta(step, slot):
        return (pltpu.make_async_copy(idx_hbm.at[pl.ds(step * td, td)], idx_s[slot], sem_m.at[0, slot]),
                pltpu.make_async_copy(rank_hbm.at[pl.ds(step * td, td)], rank_s[slot], sem_m.at[1, slot]))

    def wait_rows(slot):
        pltpu.make_async_copy(xs_hbm.at[pl.ds(0, na)], xs_hbm.at[pl.ds(0, na)], sem_r.at[slot]).wait()

    @pl.when(i == 0)
    def _():
        zblk[...] = jnp.zeros_like(zblk)
        total = pstart_ref[n_exp - 1] + pad_ref[n_exp - 1]
        n_tail = (xs_hbm.shape[0] - total) // bm

        def zero_copy(start):
            return pltpu.make_async_copy(zblk, xs_hbm.at[pl.ds(start, bm)], sem_z)

        def fill(e, _):
            @pl.when(pad_ref[e] > 0)
            def _():
                zero_copy(pstart_ref[e] + pad_ref[e] - bm).start()
            return 0

        def drain(e, _):
            @pl.when(pad_ref[e] > 0)
            def _():
                zero_copy(0).wait()
            return 0

        lax.fori_loop(0, n_exp, fill, 0)
        lax.fori_loop(0, n_tail, lambda j, c: (zero_copy(total + j * bm).start(), c)[1], 0)
        lax.fori_loop(0, n_exp, drain, 0)
        lax.fori_loop(0, n_tail, lambda j, c: (zero_copy(0).wait(), c)[1], 0)
        for c in meta(0, 0):
            c.start()

    def step(slot):
        @pl.when(i + 1 < n_steps)
        def _():
            for c in meta(i + 1, 1 - slot):
                c.start()

        for c in meta(i, slot):
            c.wait()
        rows[slot][...] = h_ref[...].reshape(td, 1, d)

        def row(t, _):
            for k in range(TOP_K):
                dst = pstart_ref[idx_s[slot][t, k]] + rank_s[slot][t, k]
                pltpu.make_async_copy(rows[slot].at[t], xs_hbm.at[dst], sem_r.at[slot]).start(priority=k % 2)
            return 0

        lax.fori_loop(0, td, row, 0)

        @pl.when(i > 0)
        def _():
            wait_rows(1 - slot)

        @pl.when(i == n_steps - 1)
        def _():
            wait_rows(slot)

    for slot in range(2):
        pl.when(i % 2 == slot)(functools.partial(step, slot))


def _dispatch(h, idx, rank, pstart, padded, cap, bm):
    n, d = h.shape
    n_exp = pstart.shape[0]
    td = _pick(n, (256, 128, 64, 32))
    grid_spec = pltpu.PrefetchScalarGridSpec(
        num_scalar_prefetch=2,
        grid=(n // td,),
        in_specs=[pl.BlockSpec((td, d), lambda i, ps, ct: (i, 0)),
                  pl.BlockSpec(memory_space=pl.ANY),
                  pl.BlockSpec(memory_space=pl.ANY)],
        out_specs=pl.BlockSpec(memory_space=pl.ANY),
        scratch_shapes=[pltpu.SMEM((td, TOP_K), I32)] * 4
        + [pltpu.VMEM((td, 1, d), F32), pltpu.VMEM((td, 1, d), F32), pltpu.VMEM((bm, 1, d), F32),
           pltpu.SemaphoreType.DMA((2, 2)), pltpu.SemaphoreType.DMA((2,)), pltpu.SemaphoreType.DMA],
    )
    return pl.pallas_call(
        functools.partial(_dispatch_kernel, n_exp=n_exp),
        out_shape=jax.ShapeDtypeStruct((cap, 1, d), F32),
        grid_spec=grid_spec,
        compiler_params=_cparams(("arbitrary",)),
        name="moe_dispatch",
    )(pstart, padded, h, idx, rank)


def _experts_kernel(be_ref, nvb_ref, xs_ref, w1_ref, b1_ref, w2_ref, b2_ref, ys_ref, w1b, w2b, x2d):
    i = pl.program_id(0)
    valid = i < nvb_ref[0]
    f, d = w2b.shape
    bm = x2d.shape[0]
    fresh = jnp.logical_or(i == 0, be_ref[i] != be_ref[jnp.maximum(i - 1, 0)])

    @pl.when(jnp.logical_and(valid, fresh))
    def _():
        w1b[...] = w1_ref[0, 0].astype(BF16)
        w2b[...] = w2_ref[0, 0].astype(BF16)

    @pl.when(valid)
    def _():
        x2d[...] = xs_ref[...].reshape(bm, d)
        h = jnp.dot(x2d[...].astype(BF16), w1b[...], preferred_element_type=F32) + b1_ref[0, 0]
        gate = jnp.minimum(h[:, :f], SWIGLU_LIMIT)
        up = jnp.clip(h[:, f:], -SWIGLU_LIMIT, SWIGLU_LIMIT)
        act = (up + 1.0) * gate * jax.nn.sigmoid(SWIGLU_ALPHA * gate)
        y = jnp.dot(act.astype(BF16), w2b[...], preferred_element_type=F32) + b2_ref[0, 0]
        ys_ref[...] = y.reshape(bm, 1, d)

    @pl.when(jnp.logical_not(valid))
    def _():
        ys_ref[...] = jnp.zeros_like(ys_ref)


def _experts(xs, block_e, n_valid, w1, b1, w2, b2, layer, n_blocks, bm):
    d = xs.shape[-1]
    _, e, _, f2 = w1.shape
    f = f2 // 2
    b1 = b1.reshape(-1, e, 1, f2)
    b2 = b2.reshape(-1, e, 1, d)
    grid_spec = pltpu.PrefetchScalarGridSpec(
        num_scalar_prefetch=2,
        grid=(n_blocks,),
        in_specs=[pl.BlockSpec((bm, 1, d), lambda i, be, nv: (jnp.minimum(i, nv[0] - 1), 0, 0)),
                  pl.BlockSpec((1, 1, d, f2), lambda i, be, nv: (layer, be[i], 0, 0)),
                  pl.BlockSpec((1, 1, 1, f2), lambda i, be, nv: (layer, be[i], 0, 0)),
                  pl.BlockSpec((1, 1, f, d), lambda i, be, nv: (layer, be[i], 0, 0)),
                  pl.BlockSpec((1, 1, 1, d), lambda i, be, nv: (layer, be[i], 0, 0))],
        out_specs=pl.BlockSpec((bm, 1, d), lambda i, be, nv: (i, 0, 0)),
        scratch_shapes=[pltpu.VMEM((d, f2), BF16), pltpu.VMEM((f, d), BF16), pltpu.VMEM((bm, d), F32)],
    )
    return pl.pallas_call(
        _experts_kernel,
        out_shape=jax.ShapeDtypeStruct((n_blocks * bm, 1, d), F32),
        grid_spec=grid_spec,
        compiler_params=_cparams(("arbitrary",)),
        name="moe_experts",
    )(block_e, n_valid, xs, w1, b1, w2, b2)


def _combine_kernel(pstart_ref, h_ref, gate_ref, idx_hbm, rank_hbm, ys_hbm, g_ref, b_ref, o_ref,
                    idx_s0, idx_s1, rank_s0, rank_s1, buf0, buf1, y2d, sem_m, sem_r, *, alpha, blk_off):
    i = pl.program_id(0)
    n_steps = pl.num_programs(0)
    tf, d = h_ref.shape
    idx_s, rank_s, buf = (idx_s0, idx_s1), (rank_s0, rank_s1), (buf0, buf1)

    def meta(step, slot):
        rows = pl.ds((step + blk_off) * tf, tf)
        return (pltpu.make_async_copy(idx_hbm.at[rows], idx_s[slot], sem_m.at[0, slot]),
                pltpu.make_async_copy(rank_hbm.at[rows], rank_s[slot], sem_m.at[1, slot]))

    def gather(slot):
        def row(t, _):
            for k in range(TOP_K):
                src = pstart_ref[idx_s[slot][t, k]] + rank_s[slot][t, k]
                pltpu.make_async_copy(ys_hbm.at[src], buf[slot].at[k * tf + t], sem_r.at[slot]).start()
            return 0

        lax.fori_loop(0, tf, row, 0)

    @pl.when(i == 0)
    def _():
        for c in meta(0, 0):
            c.start()
        for c in meta(0, 0):
            c.wait()
        gather(0)

        @pl.when(n_steps > 1)
        def _():
            for c in meta(1, 1):
                c.start()

    def step(slot):
        @pl.when(i + 2 < n_steps)
        def _():
            for c in meta(i + 2, slot):
                c.start()

        @pl.when(i + 1 < n_steps)
        def _():
            for c in meta(i + 1, 1 - slot):
                c.wait()
            gather(1 - slot)

        pltpu.make_async_copy(buf[slot], buf[slot], sem_r.at[slot]).wait()
        y = alpha * h_ref[...]
        gates = gate_ref[...]
        for k in range(TOP_K):
            y2d[...] = buf[slot][pl.ds(k * tf, tf)].reshape(tf, d)
            y = y + gates[:, k:k + 1] * y2d[...]
        o_ref[...] = _layer_norm(y, g_ref[...], b_ref[...])

    for slot in range(2):
        pl.when(i % 2 == slot)(functools.partial(step, slot))


def _combine(h, gates, idx, rank, ys, pstart, g, beta, alpha, n, row_off):
    d = h.shape[1]
    tf = _pick(math.gcd(n, row_off) if row_off else n, (256, 128, 64, 32))
    bo = row_off // tf
    grid_spec = pltpu.PrefetchScalarGridSpec(
        num_scalar_prefetch=1,
        grid=(n // tf,),
        in_specs=[pl.BlockSpec((tf, d), lambda i, ps: (i + bo, 0)),
                  pl.BlockSpec((tf, TOP_K), lambda i, ps: (i + bo, 0)),
                  pl.BlockSpec(memory_space=pl.ANY),
                  pl.BlockSpec(memory_space=pl.ANY),
                  pl.BlockSpec(memory_space=pl.ANY),
                  pl.BlockSpec((1, d), lambda i, ps: (0, 0)),
                  pl.BlockSpec((1, d), lambda i, ps: (0, 0))],
        out_specs=pl.BlockSpec((tf, d), lambda i, ps: (i, 0)),
        scratch_shapes=[pltpu.SMEM((tf, TOP_K), I32)] * 4
        + [pltpu.VMEM((TOP_K * tf, 1, d), F32), pltpu.VMEM((TOP_K * tf, 1, d), F32), pltpu.VMEM((tf, d), F32),
           pltpu.SemaphoreType.DMA((2, 2)), pltpu.SemaphoreType.DMA((2,))],
    )
    return pl.pallas_call(
        functools.partial(_combine_kernel, alpha=alpha, blk_off=bo),
        out_shape=jax.ShapeDtypeStruct((n, d), F32),
        grid_spec=grid_spec,
        compiler_params=_cparams(("arbitrary",)),
        name="moe_combine_ln",
    )(pstart, h, gates, idx, rank, ys, g.reshape(1, -1), beta.reshape(1, -1))


def _moe_ln(h, w_router, b_router, w1, b1, w2, b2, layer, g, beta, alpha, splits):
    n, d = h.shape
    n_exp = w_router.shape[1]
    ep = -(-n_exp // LANES) * LANES
    bm = EXPERT_BLOCK
    w_pad = jnp.zeros((d, ep), F32).at[:, :n_exp].set(w_router)
    b_pad = jnp.full((1, ep), NEG_BIG, F32).at[0, :n_exp].set(b_router)
    idx, gates, rank, counts = _router(h, w_pad, b_pad)
    counts = counts[0, :n_exp]
    padded = (counts + bm - 1) // bm * bm
    ends = jnp.cumsum(padded)
    pstart = (ends - padded).astype(I32)
    n_blocks = -(-(n * TOP_K + n_exp * (bm - 1)) // bm)
    starts = jnp.arange(n_blocks, dtype=I32) * bm
    block_e = jnp.minimum(jnp.sum(starts[:, None] >= ends[None, :], axis=1), n_exp - 1).astype(I32)
    n_valid = (ends[-1:] // bm).astype(I32)
    xs = _dispatch(h, idx, rank, pstart, padded.astype(I32), n_blocks * bm, bm)
    ys = _experts(xs, block_e, n_valid, w1, b1, w2, b2, layer, n_blocks, bm)
    return [_combine(h, gates, idx, rank, ys, pstart, g, beta, alpha, rows, off)
            for off, rows in splits]


def kernel(x_prompt, x_sample, state_conv, cache_k, cache_v, page_table, w_pw1, b_pw1, dw_w, dw_b, cn_g, cn_b, w_pw2, b_pw2, w_qkv, b_qkv, w_o, ln_mix_g, ln_mix_b, ln_ffn_g, ln_ffn_b, w_router, b_router, w_e1, b_e1, w_e2, b_e2):
    b, s, d = x_prompt.shape
    db, t, _ = x_sample.shape
    depth = ln_mix_g.shape[0]
    n_heads, dh = cache_k.shape[3], cache_k.shape[4]
    width = dw_w.shape[1]
    n_p, n_s = b * s, db * t
    n = n_p + n_s
    alpha = (2 * depth) ** 0.25
    hist = -(-(width - 1) // SUBLANES) * SUBLANES

    xp, xs_ = x_prompt.reshape(n_p, d), x_sample.reshape(n_s, d)
    h = None
    conv_p, conv_s, k_p, v_p, k_s, v_s = [], [], [], [], [], []
    for i in range(depth):
        j = i // 2
        last = i == depth - 1
        if i % 2 == 0:
            w1_bf = w_pw1[j].astype(BF16)
            up = _mm_glu(xp if h is None else h[:n_p], w1_bf, b_pw1[j])
            us = _mm_glu(xs_ if h is None else h[n_p:], w1_bf, b_pw1[j])
            state = jnp.pad(state_conv[j].astype(F32), ((0, 0), (hist - (width - 1), 0), (0, 0)))
            yp, cp = _conv(up.reshape(b, s, d), None, dw_w[j], dw_b[j], cn_g[j], cn_b[j])
            ys, cs = _conv(us.reshape(db, t, d), state, dw_w[j], dw_b[j], cn_g[j], cn_b[j])
            conv_p.append(cp)
            conv_s.append(cs)
            w2_bf = w_pw2[j].astype(BF16)
            res = [(xp, 0), (xs_, 0)] if h is None else [(h, 0), (h, n_p)]
            segs = [(yp.reshape(n_p, d), *res[0]), (ys.reshape(n_s, d), *res[1])]
            hm = _mm_res_ln(segs, w2_bf, b_pw2[j], ln_mix_g[i], ln_mix_b[i], alpha)
        else:
            wq_bf = w_qkv[j].astype(BF16)
            qp, kp_t, vp_t = _mm_qkv_t(h, wq_bf, b_qkv[j], b, s)
            qs, ks, vs = _mm_qkv(h, wq_bf, b_qkv[j], n_s, n_p)
            op = _attn_prompt(qp, kp_t, vp_t, dh)
            os_ = _attn_sample(qs, ks, vs, cache_k, cache_v, j, page_table)
            k_p.append(jnp.transpose(kp_t.reshape(b, n_heads, dh, s), (0, 3, 1, 2)))
            v_p.append(jnp.transpose(vp_t.reshape(b, n_heads, dh, s), (0, 3, 1, 2)))
            k_s.append(ks.reshape(db, t, n_heads, dh))
            v_s.append(vs.reshape(db, t, n_heads, dh))
            hm = _mm_res_ln([(op, h, 0), (os_, h, n_p)], w_o[j].astype(BF16), jnp.zeros((d,), F32),
                            ln_mix_g[i], ln_mix_b[i], alpha)
        splits = [(0, n_p), (n_p, n_s)] if last else [(0, n)]
        outs = _moe_ln(hm, w_router[i], b_router[i], w_e1, b_e1, w_e2, b_e2, i,
                       ln_ffn_g[i], ln_ffn_b[i], alpha, splits)
        if last:
            y_p, y_s = outs
        else:
            h = outs[0]
    return (y_p.reshape(b, s, d), y_s.reshape(db, t, d), jnp.stack(conv_p), jnp.stack(conv_s),
            jnp.stack(k_p), jnp.stack(v_p), jnp.stack(k_s), jnp.stack(v_s))
```
